```python
import jax, jax.numpy as jnp
from jax import lax
import numpy as np

D_MODEL = 1024
BATCH = 16
SEQ = 256
DEPTH = 1
DEC_BATCH = 8
DEC_SEQ = 2048
PAST_LEN = 512

GRID_W = 64
N_HEADS = 16
N_KV_HEADS = 4
HEAD_DIM = 64
ATT_WIDTH = N_HEADS * HEAD_DIM
KV_WIDTH = N_KV_HEADS * HEAD_DIM
CONV_WIDTH = D_MODEL
CONV_K = 3
D_FF = 2816
Q_BLOCK = 128
ROPE_THETA = 10000.0
AXIS_DIM = HEAD_DIM // 2
N_ADA = 6
EPS = 1e-6
IN_SPLITS = (ATT_WIDTH, KV_WIDTH, KV_WIDTH, CONV_WIDTH, CONV_WIDTH, CONV_WIDTH, D_MODEL, D_MODEL)
IN_WIDTH = 3 * CONV_WIDTH + ATT_WIDTH + 2 * KV_WIDTH + 2 * D_MODEL

kernel_name = "hybrid_diffusion_prefix_gqa_shortconv_step"


def rms_norm(x, g):
    xf = x.astype(jnp.float32)
    y = xf * lax.rsqrt(jnp.mean(xf * xf, axis=-1, keepdims=True) + EPS)
    return (y * g.astype(jnp.float32)).astype(x.dtype)


def dwconv3(x, w):
    xp = jnp.pad(x, ((0, 0), (1, 1), (0, 0)))
    return w[0] * xp[:, :-2] + w[1] * xp[:, 1:-1] + w[2] * xp[:, 2:]


def axial_rope_tables(n):
    rows = n // GRID_W
    row = jnp.repeat(jnp.arange(rows, dtype=jnp.float32), GRID_W)
    col = jnp.tile(jnp.arange(GRID_W, dtype=jnp.float32), rows)
    inv = jnp.power(ROPE_THETA, -jnp.arange(0, AXIS_DIM, 2, dtype=jnp.float32) / AXIS_DIM)
    ang_r = row[:, None] * inv[None, :]
    ang_c = col[:, None] * inv[None, :]
    return (jnp.cos(ang_r), jnp.sin(ang_r), jnp.cos(ang_c), jnp.sin(ang_c))


def _rotate(xh, cos, sin):
    half = xh.shape[-1] // 2
    x1, x2 = xh[..., :half], xh[..., half:]
    c = cos[None, :, None, :]
    s = sin[None, :, None, :]
    return jnp.concatenate([x1 * c - x2 * s, x1 * s + x2 * c], axis=-1)


def apply_axial_rope(x, tabs):
    cr, sr, cc, sc = tabs
    xf = x.astype(jnp.float32)
    out = jnp.concatenate([_rotate(xf[..., :AXIS_DIM], cr, sr), _rotate(xf[..., AXIS_DIM:], cc, sc)], axis=-1)
    return out.astype(x.dtype)


def block_attention(q, k, v):
    B, N, H, D = q.shape
    KV = k.shape[2]
    G = H // KV
    qb_len = Q_BLOCK if N % Q_BLOCK == 0 else N
    nb = N // qb_len
    scale = D ** -0.5
    qb = q.reshape(B, nb, qb_len, KV, G, D).transpose(1, 0, 2, 3, 4, 5)

    def one_block(qblk):
        s = jnp.einsum('bqkgd,btkd->bkgqt', qblk, k).astype(jnp.float32) * scale
        p = jax.nn.softmax(s, axis=-1).astype(v.dtype)
        return jnp.einsum('bkgqt,btkd->bqkgd', p, v)

    o = lax.map(one_block, qb)
    return o.transpose(1, 0, 2, 3, 4, 5).reshape(B, N, H * D)


def mixer(u, p, rope_tabs, ctx_k, ctx_v):
    B, N, _ = u.shape
    z = u @ p["w_in"]
    idx = np.cumsum(np.array(IN_SPLITS))[:-1].tolist()
    q, k, v, b_gate, c_gate, x_in, g_att, g_conv = jnp.split(z, idx, axis=-1)
    q = rms_norm(q.reshape(B, N, N_HEADS, HEAD_DIM), p["q_norm"])
    k = rms_norm(k.reshape(B, N, N_KV_HEADS, HEAD_DIM), p["k_norm"])
    v = v.reshape(B, N, N_KV_HEADS, HEAD_DIM)
    if rope_tabs is None:
        q_used, keys, vals = q, k, v
    else:
        q_used = apply_axial_rope(q, rope_tabs)
        keys = jnp.concatenate([ctx_k.astype(k.dtype), apply_axial_rope(k, rope_tabs)], axis=1)
        vals = jnp.concatenate([ctx_v.astype(v.dtype), v], axis=1)
    att = block_attention(q_used, keys, vals) @ p["w_att_out"]
    conv = (b_gate * dwconv3(c_gate * x_in, p["conv_w"])) @ p["w_conv_out"]
    merged = jax.nn.sigmoid(g_att) * att + jax.nn.sigmoid(g_conv) * conv
    return merged @ p["w_o"], k, v


def conv_ffn(u, p):
    up = dwconv3(u @ p["w_up"], p["conv_ffn"])
    g, val = jnp.split(up, 2, axis=-1)
    return (jax.nn.silu(g) * val) @ p["w_down"]


def trunk_layer(h, ada, p, rope_tabs, ctx_k, ctx_v):
    sh1, sc1, g1, sh2, sc2, g2 = jnp.split(ada, N_ADA, axis=-1)
    u = rms_norm(h, p["g_pre1"]) * (1 + sc1) + sh1
    mo, k, v = mixer(u, p, rope_tabs, ctx_k, ctx_v)
    h = h + g1 * rms_norm(mo, p["g_post1"])
    u = rms_norm(h, p["g_pre2"]) * (1 + sc2) + sh2
    h = h + g2 * rms_norm(conv_ffn(u, p), p["g_post2"])
    return h, k, v


def setup_inputs(seed: int = 0) -> dict:
    key = jax.random.key(seed)
    ks = jax.random.split(key, 24)
    f32 = jnp.float32
    nrm = lambda k, shape, s: (jax.random.normal(k, shape, f32) * s)
    gain = lambda k, shape: 1.0 + 0.05 * jax.random.normal(k, shape, f32)
    return {
        "x_prompt": nrm(ks[0], (BATCH, SEQ, D_MODEL), 1.0),
        "x_sample": nrm(ks[1], (DEC_BATCH, DEC_SEQ, D_MODEL), 1.0),
        "cache_k": nrm(ks[2], (DEC_BATCH, DEPTH, PAST_LEN, N_KV_HEADS, HEAD_DIM), 1.0),
        "cache_v": nrm(ks[3], (DEC_BATCH, DEPTH, PAST_LEN, N_KV_HEADS, HEAD_DIM), 1.0),
        "c": nrm(ks[4], (DEC_BATCH, D_MODEL), 1.0),
        "c_ctx": nrm(ks[5], (D_MODEL,), 1.0),
        "w_ada": nrm(ks[6], (DEPTH, D_MODEL, N_ADA * D_MODEL), 0.5 * D_MODEL ** -0.5),
        "b_ada": nrm(ks[7], (DEPTH, N_ADA * D_MODEL), 0.01),
        "g_pre1": gain(ks[8], (DEPTH, D_MODEL)),
        "g_post1": gain(ks[9], (DEPTH, D_MODEL)),
        "g_pre2": gain(ks[10], (DEPTH, D_MODEL)),
        "g_post2": gain(ks[11], (DEPTH, D_MODEL)),
        "w_in": nrm(ks[12], (DEPTH, D_MODEL, IN_WIDTH), D_MODEL ** -0.5),
        "q_norm": gain(ks[13], (DEPTH, HEAD_DIM)),
        "k_norm": gain(ks[14], (DEPTH, HEAD_DIM)),
        "w_att_out": nrm(ks[15], (DEPTH, ATT_WIDTH, D_MODEL), ATT_WIDTH ** -0.5),
        "conv_w": nrm(ks[16], (DEPTH, CONV_K, CONV_WIDTH), CONV_K ** -0.5),
        "w_conv_out": nrm(ks[17], (DEPTH, CONV_WIDTH, D_MODEL), CONV_WIDTH ** -0.5),
        "w_o": nrm(ks[18], (DEPTH, D_MODEL, D_MODEL), D_MODEL ** -0.5),
        "w_up": nrm(ks[19], (DEPTH, D_MODEL, 2 * D_FF), D_MODEL ** -0.5),
        "conv_ffn": nrm(ks[20], (DEPTH, CONV_K, 2 * D_FF), CONV_K ** -0.5),
        "w_down": nrm(ks[21], (DEPTH, D_FF, D_MODEL), D_FF ** -0.5),
    }


def reference(x_prompt, x_sample, cache_k, cache_v, c, c_ctx, w_ada, b_ada, g_pre1, g_post1, g_pre2, g_post2,
              w_in, q_norm, k_norm, w_att_out, conv_w, w_conv_out, w_o, w_up, conv_ffn, w_down):
    rope_tabs = axial_rope_tables(x_sample.shape[1])
    h_p = x_prompt
    h_s = x_sample
    new_ks = []
    new_vs = []
    for i in range(DEPTH):
        p = {
            "g_pre1": g_pre1[i], "g_post1": g_post1[i], "g_pre2": g_pre2[i], "g_post2": g_post2[i],
            "w_in": w_in[i], "q_norm": q_norm[i], "k_norm": k_norm[i], "w_att_out": w_att_out[i],
            "conv_w": conv_w[i], "w_conv_out": w_conv_out[i], "w_o": w_o[i],
            "w_up": w_up[i], "conv_ffn": conv_ffn[i], "w_down": w_down[i],
        }
        ada_ctx = (jax.nn.silu(c_ctx) @ w_ada[i] + b_ada[i])[None, None, :]
        ada_lat = (jax.nn.silu(c) @ w_ada[i] + b_ada[i])[:, None, :]
        h_p, k_ctx, v_ctx = trunk_layer(h_p, ada_ctx, p, None, None, None)
        new_ks.append(k_ctx)
        new_vs.append(v_ctx)
        h_s, _, _ = trunk_layer(h_s, ada_lat, p, rope_tabs, cache_k[:, i], cache_v[:, i])
    new_k = jnp.stack(new_ks, axis=1)
    new_v = jnp.stack(new_vs, axis=1)
    return (h_p, h_s, new_k, new_v)
```

```python
import functools

import numpy as np
import jax
import jax.numpy as jnp
from jax.experimental import pallas as pl
from jax.experimental.pallas import tpu as pltpu

D_MODEL = 1024
N_HEADS = 16
N_KV_HEADS = 4
HEAD_DIM = 64
GROUP = N_HEADS // N_KV_HEADS
ATT_WIDTH = N_HEADS * HEAD_DIM
KV_WIDTH = N_KV_HEADS * HEAD_DIM
D_FF = 2816
GRID_W = 64
ROPE_THETA = 10000.0
AXIS_DIM = HEAD_DIM // 2
N_ADA = 6
EPS = 1e-6

OFF_Q = 0
OFF_K = OFF_Q + ATT_WIDTH
OFF_V = OFF_K + KV_WIDTH
OFF_B = OFF_V + KV_WIDTH
OFF_C = OFF_B + D_MODEL
OFF_X = OFF_C + D_MODEL
OFF_GA = OFF_X + D_MODEL
OFF_GC = OFF_GA + D_MODEL
IN_WIDTH = OFF_GC + D_MODEL

LANES = 128
MXU_N = 256
HALO = 16
ADA_ROWS = 16
VMEM_LIMIT = 56 * 1024 * 1024

BF16 = jnp.bfloat16
F32 = jnp.float32


def _dot(a, b):
    return jnp.dot(a, b, preferred_element_type=F32)


def _resident(shape):
    nd = len(shape)
    return pl.BlockSpec(shape, lambda *_: (0,) * nd, pipeline_mode=pl.Buffered(1))


def _params(n_axes):
    return pltpu.CompilerParams(dimension_semantics=("arbitrary",) * n_axes, vmem_limit_bytes=VMEM_LIMIT)


def _ada_kernel(c_ref, w_ref, b_ref, o_ref):
    c = c_ref[...]
    s = (c * jax.nn.sigmoid(c)).astype(BF16)
    o_ref[...] = _dot(s, w_ref[...].astype(BF16)) + b_ref[...]


def _ada(cc, w_ada, b_ada):
    n = w_ada.shape[1]
    tn = D_MODEL
    return pl.pallas_call(
        _ada_kernel,
        grid=(n // tn,),
        in_specs=[
            pl.BlockSpec((ADA_ROWS, D_MODEL), lambda j: (0, 0)),
            pl.BlockSpec((D_MODEL, tn), lambda j: (0, j)),
            pl.BlockSpec((1, tn), lambda j: (0, j)),
        ],
        out_specs=pl.BlockSpec((ADA_ROWS, tn), lambda j: (0, j)),
        out_shape=jax.ShapeDtypeStruct((ADA_ROWS, n), F32),
        compiler_params=_params(1),
        name="ada",
    )(cc, w_ada, b_ada.reshape(1, n))


def _rms_rows(x):
    return x * jax.lax.rsqrt(jnp.mean(x * x, axis=-1, keepdims=True) + EPS)


def _head_norm(z, seg_ref, gain):
    ms = _dot((z * z).astype(BF16), seg_ref[...])
    return z * jax.lax.rsqrt(ms + EPS) * gain


def _rope(z, cos, sin, first_half):
    partner = jnp.where(first_half, pltpu.roll(z, LANES - AXIS_DIM // 2, axis=1), pltpu.roll(z, AXIS_DIM // 2, axis=1))
    return z * cos + partner * sin


def _in_proj_kernel(rope, x_ref, sh_ref, sc_ref, gpre_ref, w_ref, qg_ref, kg_ref, seg_ref, *rest):
    if rope:
        cos_ref, sin_ref, q_ref, k_ref, v_ref, b_ref, y_ref, sa_ref, sg_ref = rest
        cos = cos_ref[...]
        sin = sin_ref[...]
        lane = jax.lax.broadcasted_iota(jnp.int32, (1, LANES), 1)
        first_half = (lane % AXIS_DIM) < (AXIS_DIM // 2)
    else:
        q_ref, k_ref, v_ref, b_ref, y_ref, sa_ref, sg_ref = rest

    x = x_ref[...]
    mod = gpre_ref[...] * (1.0 + sc_ref[0])
    u = (_rms_rows(x) * mod + sh_ref[0]).astype(BF16)

    def proj(off, j):
        lo = off + j * MXU_N
        return _dot(u, w_ref[:, lo:lo + MXU_N])

    def rotate(z):
        if not rope:
            return z
        return jnp.concatenate(
            [_rope(z[:, h * LANES:(h + 1) * LANES], cos, sin, first_half) for h in range(MXU_N // LANES)], axis=1)

    for j in range(ATT_WIDTH // MXU_N):
        q = _head_norm(proj(OFF_Q, j), seg_ref, qg_ref[...])
        q_ref[:, j * MXU_N:(j + 1) * MXU_N] = rotate(q).astype(BF16)

    k = _head_norm(proj(OFF_K, 0), seg_ref, kg_ref[...])
    k_ref[...] = rotate(k).astype(k_ref.dtype)
    v_ref[...] = proj(OFF_V, 0).astype(v_ref.dtype)

    for j in range(D_MODEL // MXU_N):
        cols = slice(j * MXU_N, (j + 1) * MXU_N)
        b_ref[:, cols] = proj(OFF_B, j).astype(BF16)
        y_ref[:, cols] = (proj(OFF_C, j) * proj(OFF_X, j)).astype(BF16)
        sa_ref[:, cols] = jax.nn.sigmoid(proj(OFF_GA, j)).astype(BF16)
        sg_ref[:, cols] = jax.nn.sigmoid(proj(OFF_GC, j)).astype(BF16)


def _in_proj(x, ada3, ada_row, seq_len, tm, w_in, g_pre1, qg, kg, seg, rope_tabs, kv_dtype):
    n = x.shape[0]
    tiles_per_seq = seq_len // tm if seq_len >= tm else 1
    rope = rope_tabs is not None
    row = lambda i: (i, 0)
    in_specs = [
        pl.BlockSpec((tm, D_MODEL), row),
        pl.BlockSpec((1, 1, D_MODEL), lambda i: (ada_row(i), 0, 0)),
        pl.BlockSpec((1, 1, D_MODEL), lambda i: (ada_row(i), 0, 1)),
        _resident((1, D_MODEL)),
        _resident((D_MODEL, IN_WIDTH)),
        _resident((1, MXU_N)),
        _resident((1, MXU_N)),
        _resident((MXU_N, MXU_N)),
    ]
    args = [x, ada3, ada3, g_pre1, w_in, qg, kg, seg]
    if rope:
        in_specs += [pl.BlockSpec((tm, LANES), lambda i: (i % tiles_per_seq, 0))] * 2
        args += list(rope_tabs)
    wide = jax.ShapeDtypeStruct((n, D_MODEL), BF16)
    kv = jax.ShapeDtypeStruct((n, KV_WIDTH), kv_dtype)
    return pl.pallas_call(
        functools.partial(_in_proj_kernel, rope),
        grid=(n // tm,),
        in_specs=in_specs,
        out_specs=[pl.BlockSpec((tm, D_MODEL), row), pl.BlockSpec((tm, KV_WIDTH), row), pl.BlockSpec((tm, KV_WIDTH), row)]
        + [pl.BlockSpec((tm, D_MODEL), row)] * 4,
        out_shape=[wide, kv, kv, wide, wide, wide, wide],
        compiler_params=_params(1),
        name="in_proj_rope" if rope else "in_proj",
    )(*args)


def _attn_kernel(n_groups, q_ref, kt_ref, v_ref, o_ref):
    for g in range(n_groups):
        kt = kt_ref[0, g]
        v = v_ref[0, g]
        outs = []
        for h in range(GROUP):
            lo = g * MXU_N + h * HEAD_DIM
            s = _dot(q_ref[0, :, lo:lo + HEAD_DIM], kt)
            m = jnp.max(s, axis=-1, keepdims=True)
            p = jnp.exp(s - m)
            l = jnp.sum(p, axis=-1, keepdims=True)
            outs.append(_dot(p.astype(BF16), v) * (1.0 / l))
        o_ref[0, :, g * MXU_N:(g + 1) * MXU_N] = jnp.concatenate(outs, axis=1).astype(BF16)


def _attention(q, kt, v, tq, groups_per_step):
    bsz, n, _ = q.shape
    t = kt.shape[-1]
    gs = groups_per_step
    width = gs * MXU_N
    return pl.pallas_call(
        functools.partial(_attn_kernel, gs),
        grid=(bsz, N_KV_HEADS // gs, n // tq),
        in_specs=[
            pl.BlockSpec((1, tq, width), lambda b, g, i: (b, i, g)),
            pl.BlockSpec((1, gs, HEAD_DIM, t), lambda b, g, i: (b, g, 0, 0)),
            pl.BlockSpec((1, gs, t, HEAD_DIM), lambda b, g, i: (b, g, 0, 0)),
        ],
        out_specs=pl.BlockSpec((1, tq, width), lambda b, g, i: (b, i, g)),
        out_shape=jax.ShapeDtypeStruct(q.shape, BF16),
        compiler_params=_params(3),
        name=f"attn_t{t}",
    )(q, kt, v)


def _shifted_rows(x, prev_row, next_row):
    tm = x.shape[0]
    t = jax.lax.broadcasted_iota(jnp.int32, (tm, 1), 0)
    before = jnp.where(t == 0, prev_row, pltpu.roll(x, 1, axis=0))
    after = jnp.where(t == tm - 1, next_row, pltpu.roll(x, tm - 1, axis=0))
    return before, after


def _mix_out_kernel(tiles_per_seq, att_ref, b_ref, y_ref, yp_ref, yn_ref, sa_ref, sg_ref, h_ref, g1_ref, sh2_ref,
                    sc2_ref, gpost_ref, gpre_ref, cw_ref, wa_ref, wc_ref, wo_ref, h1_ref, u2_ref):
    i = pl.program_id(0)
    has_prev = (i % tiles_per_seq != 0).astype(F32)
    has_next = (i % tiles_per_seq != tiles_per_seq - 1).astype(F32)
    y = y_ref[...].astype(F32)
    y_prev = yp_ref[HALO - 1:HALO, :].astype(F32) * has_prev
    y_next = yn_ref[0:1, :].astype(F32) * has_next
    before, after = _shifted_rows(y, y_prev, y_next)
    conv = cw_ref[0:1, :] * before + cw_ref[1:2, :] * y + cw_ref[2:3, :] * after
    conv_in = (b_ref[...].astype(F32) * conv).astype(BF16)

    att = _dot(att_ref[...], wa_ref[...])
    cnv = _dot(conv_in, wc_ref[...])
    merged = (sa_ref[...].astype(F32) * att + sg_ref[...].astype(F32) * cnv).astype(BF16)
    mo = _dot(merged, wo_ref[...])

    h1 = h_ref[...] + g1_ref[0] * (_rms_rows(mo) * gpost_ref[...])
    h1_ref[...] = h1
    u2 = _rms_rows(h1) * (gpre_ref[...] * (1.0 + sc2_ref[0])) + sh2_ref[0]
    u2_ref[...] = u2.astype(BF16)


def _halo_specs(tm, n_rows, width):
    per = tm // HALO
    last = n_rows // HALO - 1
    prev = pl.BlockSpec((HALO, width), lambda i: (jnp.maximum(i * per - 1, 0), 0))
    nxt = pl.BlockSpec((HALO, width), lambda i: (jnp.minimum((i + 1) * per, last), 0))
    return prev, nxt


def _mix_out(att, b, y, sa, sg, h, ada3, ada_row, seq_len, tm, g_post1, g_pre2, conv_w, w_att_out, w_conv_out, w_o):
    n = h.shape[0]
    assert seq_len % tm == 0
    row = lambda i: (i, 0)
    wide = pl.BlockSpec((tm, D_MODEL), row)
    prev, nxt = _halo_specs(tm, n, D_MODEL)
    ada = lambda k: pl.BlockSpec((1, 1, D_MODEL), lambda i: (ada_row(i), 0, k))
    sq = (D_MODEL, D_MODEL)
    return pl.pallas_call(
        functools.partial(_mix_out_kernel, seq_len // tm),
        grid=(n // tm,),
        in_specs=[wide, wide, wide, prev, nxt, wide, wide, wide, ada(2), ada(3), ada(4),
                  _resident((1, D_MODEL)), _resident((1, D_MODEL)), _resident((3, D_MODEL)),
                  _resident(sq), _resident(sq), _resident(sq)],
        out_specs=[wide, wide],
        out_shape=[jax.ShapeDtypeStruct((n, D_MODEL), F32), jax.ShapeDtypeStruct((n, D_MODEL), BF16)],
        compiler_params=_params(1),
        name=f"mix_out_s{seq_len}",
    )(att, b, y, y, y, sa, sg, h, ada3, ada3, ada3, g_post1, g_pre2, conv_w, w_att_out, w_conv_out, w_o)


FF_CHUNK = D_FF // 2


def _ffn_kernel(tiles_per_seq, u_ref, up_ref, un_ref, h_ref, g2_ref, gpost_ref, cw_ref, wup_ref, wdn_ref, o_ref, ext_ref):
    i = pl.program_id(0)
    tm = u_ref.shape[0]
    has_prev = i % tiles_per_seq != 0
    has_next = i % tiles_per_seq != tiles_per_seq - 1
    zeros = jnp.zeros((HALO, D_MODEL), BF16)
    ext_ref[0:HALO, :] = jnp.where(has_prev, up_ref[...], zeros)
    ext_ref[HALO:HALO + tm, :] = u_ref[...]
    ext_ref[HALO + tm:, :] = jnp.where(has_next, un_ref[...], zeros)
    ext = ext_ref[...]
    rows = tm + 2 * HALO

    def conv_up(col):
        z = _dot(ext, wup_ref[:, col:col + FF_CHUNK])
        w = cw_ref[:, col:col + FF_CHUNK]
        before = pltpu.roll(z, 1, axis=0)[HALO:HALO + tm]
        after = pltpu.roll(z, rows - 1, axis=0)[HALO:HALO + tm]
        return w[0:1] * before + w[1:2] * z[HALO:HALO + tm] + w[2:3] * after

    ff = None
    for c in range(D_FF // FF_CHUNK):
        gate = conv_up(c * FF_CHUNK)
        val = conv_up(D_FF + c * FF_CHUNK)
        act = (gate * jax.nn.sigmoid(gate) * val).astype(BF16)
        part = _dot(act, wdn_ref[c * FF_CHUNK:(c + 1) * FF_CHUNK, :])
        ff = part if ff is None else ff + part

    o_ref[...] = h_ref[...] + g2_ref[0] * (_rms_rows(ff) * gpost_ref[...])


def _ffn(u2, h1, ada3, ada_row, seq_len, tm, g_post2, conv_ffn, w_up, w_down):
    n = h1.shape[0]
    assert seq_len % tm == 0
    row = lambda i: (i, 0)
    wide = pl.BlockSpec((tm, D_MODEL), row)
    prev, nxt = _halo_specs(tm, n, D_MODEL)
    return pl.pallas_call(
        functools.partial(_ffn_kernel, seq_len // tm),
        grid=(n // tm,),
        in_specs=[wide, prev, nxt, wide, pl.BlockSpec((1, 1, D_MODEL), lambda i: (ada_row(i), 0, 5)),
                  _resident((1, D_MODEL)), _resident((3, 2 * D_FF)),
                  _resident((D_MODEL, 2 * D_FF)), _resident((D_FF, D_MODEL))],
        out_specs=wide,
        out_shape=jax.ShapeDtypeStruct((n, D_MODEL), F32),
        scratch_shapes=[pltpu.VMEM((tm + 2 * HALO, D_MODEL), BF16)],
        compiler_params=_params(1),
        name=f"ffn_s{seq_len}",
    )(u2, u2, u2, h1, ada3, g_post2, conv_ffn, w_up, w_down)


def _rope_tables(n):
    pos = np.arange(n)
    inv = np.power(ROPE_THETA, -np.arange(0, AXIS_DIM, 2, dtype=np.float64) / AXIS_DIM)
    ang_r = (pos // GRID_W)[:, None] * inv[None, :]
    ang_c = (pos % GRID_W)[:, None] * inv[None, :]
    ang = np.concatenate([ang_r, ang_r, ang_c, ang_c], axis=1)
    sign = np.tile(np.concatenate([-np.ones(AXIS_DIM // 2), np.ones(AXIS_DIM // 2)]), 2)
    reps = LANES // HEAD_DIM
    cos = np.tile(np.cos(ang), (1, reps)).astype(np.float32)
    sin = np.tile(np.sin(ang) * sign[None, :], (1, reps)).astype(np.float32)
    return jnp.asarray(cos), jnp.asarray(sin)


def _segment_mean_matrix():
    head = np.arange(MXU_N) // HEAD_DIM
    return jnp.asarray((head[:, None] == head[None, :]).astype(np.float32) / HEAD_DIM, dtype=BF16)


def _layer(x, ada3, ada_row, seq_len, tiles, w, rope_tabs, cache_kv):
    bsz = x.shape[0]
    n = bsz * seq_len
    xf = x.reshape(n, D_MODEL)
    tm_in, tq, gs, tm_mix, tm_ffn = tiles
    kv_dtype = BF16 if cache_kv is not None else F32
    q, k, v, b, y, sa, sg = _in_proj(xf, ada3, lambda i: ada_row(i, tm_in), seq_len, tm_in, w["w_in"], w["g_pre1"],
                                     w["qg"], w["kg"], w["seg"], rope_tabs, kv_dtype)
    k3 = k.reshape(bsz, seq_len, KV_WIDTH).astype(BF16)
    v3 = v.reshape(bsz, seq_len, KV_WIDTH).astype(BF16)
    if cache_kv is not None:
        ck, cv = cache_kv
        k3 = jnp.concatenate([ck.reshape(bsz, -1, KV_WIDTH).astype(BF16), k3], axis=1)
        v3 = jnp.concatenate([cv.reshape(bsz, -1, KV_WIDTH).astype(BF16), v3], axis=1)
    t = k3.shape[1]
    kt = k3.reshape(bsz, t, N_KV_HEADS, HEAD_DIM).transpose(0, 2, 3, 1)
    vh = v3.reshape(bsz, t, N_KV_HEADS, HEAD_DIM).transpose(0, 2, 1, 3)
    att = _attention(q.reshape(bsz, seq_len, ATT_WIDTH), kt, vh, tq, gs).reshape(n, ATT_WIDTH)
    h1, u2 = _mix_out(att, b, y, sa, sg, xf, ada3, lambda i: ada_row(i, tm_mix), seq_len, tm_mix, w["g_post1"],
                      w["g_pre2"], w["conv_w"], w["w_att_out"], w["w_conv_out"], w["w_o"])
    out = _ffn(u2, h1, ada3, lambda i: ada_row(i, tm_ffn), seq_len, tm_ffn, w["g_post2"], w["conv_ffn"], w["w_up"],
               w["w_down"])
    return out.reshape(x.shape), k, v


def kernel(x_prompt, x_sample, cache_k, cache_v, c, c_ctx, w_ada, b_ada, g_pre1, g_post1, g_pre2, g_post2, w_in, q_norm,
           k_norm, w_att_out, conv_w, w_conv_out, w_o, w_up, conv_ffn, w_down):
    depth = w_in.shape[0]
    dec_batch, dec_seq, _ = x_sample.shape
    batch, seq, _ = x_prompt.shape
    rope_tabs = _rope_tables(dec_seq)
    seg = _segment_mean_matrix()
    cc = jnp.zeros((ADA_ROWS, D_MODEL), F32).at[0].set(c_ctx).at[1:1 + dec_batch].set(c)
    reps = MXU_N // HEAD_DIM

    h_p, h_s = x_prompt, x_sample
    new_ks, new_vs = [], []
    for i in range(depth):
        w = {
            "g_pre1": g_pre1[i][None], "g_post1": g_post1[i][None], "g_pre2": g_pre2[i][None], "g_post2": g_post2[i][None],
            "w_in": w_in[i].astype(BF16),
            "qg": jnp.tile(q_norm[i] * HEAD_DIM ** -0.5, reps)[None],
            "kg": jnp.tile(k_norm[i], reps)[None],
            "seg": seg,
            "w_att_out": w_att_out[i].astype(BF16), "conv_w": conv_w[i], "w_conv_out": w_conv_out[i].astype(BF16),
            "w_o": w_o[i].astype(BF16), "w_up": w_up[i].astype(BF16), "conv_ffn": conv_ffn[i],
            "w_down": w_down[i].astype(BF16),
        }
        ada3 = _ada(cc, w_ada[i], b_ada[i]).reshape(ADA_ROWS, N_ADA, D_MODEL).reshape(ADA_ROWS, 1, N_ADA * D_MODEL)
        h_p, k_ctx, v_ctx = _layer(h_p, ada3, lambda t, tm: 0, seq, (512, seq, N_KV_HEADS, seq, seq), w, None, None)
        new_ks.append(k_ctx.reshape(batch, seq, N_KV_HEADS, HEAD_DIM))
        new_vs.append(v_ctx.reshape(batch, seq, N_KV_HEADS, HEAD_DIM))
        h_s, _, _ = _layer(h_s, ada3, lambda t, tm: 1 + t // (dec_seq // tm), dec_seq, (512, 256, 1, 512, 512), w,
                           rope_tabs, (cache_k[:, i], cache_v[:, i]))
    return (h_p, h_s, jnp.stack(new_ks, axis=1), jnp.stack(new_vs, axis=1))
```

```python
import functools

import numpy as np
import jax
import jax.numpy as jnp
from jax.experimental import pallas as pl
from jax.experimental.pallas import tpu as pltpu

D_MODEL = 1024
N_HEADS = 16
N_KV_HEADS = 4
HEAD_DIM = 64
GROUP = N_HEADS // N_KV_HEADS
ATT_WIDTH = N_HEADS * HEAD_DIM
KV_WIDTH = N_KV_HEADS * HEAD_DIM
D_FF = 2816
GRID_W = 64
ROPE_THETA = 10000.0
AXIS_DIM = HEAD_DIM // 2
N_ADA = 6
EPS = 1e-6
LOG2E = 1.4426950408889634

OFF_Q = 0
OFF_K = OFF_Q + ATT_WIDTH
OFF_V = OFF_K + KV_WIDTH
OFF_B = OFF_V + KV_WIDTH
OFF_C = OFF_B + D_MODEL
OFF_X = OFF_C + D_MODEL
OFF_GA = OFF_X + D_MODEL
OFF_GC = OFF_GA + D_MODEL
IN_WIDTH = OFF_GC + D_MODEL

LANES = 128
MXU_N = 256
HALO = 16
ADA_ROWS = 16
VMEM_LIMIT = 56 * 1024 * 1024
KEY_CHUNK = 512

BF16 = jnp.bfloat16
F32 = jnp.float32


def _dot(a, b):
    return jnp.dot(a, b, preferred_element_type=F32)


def _resident(shape):
    nd = len(shape)
    return pl.BlockSpec(shape, lambda *_: (0,) * nd, pipeline_mode=pl.Buffered(1))


def _params(n_axes):
    return pltpu.CompilerParams(dimension_semantics=("arbitrary",) * n_axes, vmem_limit_bytes=VMEM_LIMIT)


def _ada_kernel(c_ref, w_ref, b_ref, o_ref):
    c = c_ref[...]
    s = (c * jax.nn.sigmoid(c)).astype(BF16)
    o_ref[...] = _dot(s, w_ref[...].astype(BF16)) + b_ref[...]


def _ada(cc, w_ada, b_ada):
    n = w_ada.shape[1]
    tn = D_MODEL
    return pl.pallas_call(
        _ada_kernel,
        grid=(n // tn,),
        in_specs=[
            pl.BlockSpec((ADA_ROWS, D_MODEL), lambda j: (0, 0)),
            pl.BlockSpec((D_MODEL, tn), lambda j: (0, j)),
            pl.BlockSpec((1, tn), lambda j: (0, j)),
        ],
        out_specs=pl.BlockSpec((ADA_ROWS, tn), lambda j: (0, j)),
        out_shape=jax.ShapeDtypeStruct((ADA_ROWS, n), F32),
        compiler_params=_params(1),
        name="ada",
    )(cc, w_ada, b_ada.reshape(1, n))


def _rms_rows(x):
    return x * jax.lax.rsqrt(jnp.mean(x * x, axis=-1, keepdims=True) + EPS)


def _head_norm(z, seg_ref, gain):
    ms = _dot((z * z).astype(BF16), seg_ref[...])
    return z * jax.lax.rsqrt(ms + EPS) * gain


def _rope(z, cos, sin, first_half):
    partner = jnp.where(first_half, pltpu.roll(z, LANES - AXIS_DIM // 2, axis=1), pltpu.roll(z, AXIS_DIM // 2, axis=1))
    return z * cos + partner * sin


def _in_proj_kernel(rope, x_ref, sh_ref, sc_ref, gpre_ref, w_ref, qg_ref, kg_ref, seg_ref, *rest):
    if rope:
        cos_ref, sin_ref, q_ref, k_ref, v_ref, b_ref, y_ref, sa_ref, sg_ref = rest
        cos = cos_ref[...]
        sin = sin_ref[...]
        lane = jax.lax.broadcasted_iota(jnp.int32, (1, LANES), 1)
        first_half = (lane % AXIS_DIM) < (AXIS_DIM // 2)
    else:
        q_ref, k_ref, v_ref, b_ref, y_ref, sa_ref, sg_ref = rest

    x = x_ref[...]
    mod = gpre_ref[...] * (1.0 + sc_ref[0])
    u = (_rms_rows(x) * mod + sh_ref[0]).astype(BF16)

    def proj(off, j):
        lo = off + j * MXU_N
        return _dot(u, w_ref[:, lo:lo + MXU_N])

    def rotate(z):
        if not rope:
            return z
        return jnp.concatenate(
            [_rope(z[:, h * LANES:(h + 1) * LANES], cos, sin, first_half) for h in range(MXU_N // LANES)], axis=1)

    for j in range(ATT_WIDTH // MXU_N):
        q = _head_norm(proj(OFF_Q, j), seg_ref, qg_ref[...])
        q_ref[:, j * MXU_N:(j + 1) * MXU_N] = rotate(q).astype(BF16)

    k = _head_norm(proj(OFF_K, 0), seg_ref, kg_ref[...])
    k_ref[...] = rotate(k).astype(k_ref.dtype)
    v_ref[...] = proj(OFF_V, 0).astype(v_ref.dtype)

    for j in range(D_MODEL // MXU_N):
        cols = slice(j * MXU_N, (j + 1) * MXU_N)
        b_ref[:, cols] = proj(OFF_B, j).astype(BF16)
        y_ref[:, cols] = (proj(OFF_C, j) * proj(OFF_X, j)).astype(BF16)
        sa_ref[:, cols] = jax.nn.sigmoid(proj(OFF_GA, j)).astype(BF16)
        sg_ref[:, cols] = jax.nn.sigmoid(proj(OFF_GC, j)).astype(BF16)


def _in_proj(x, ada3, ada_row, seq_len, tm, w_in, g_pre1, qg, kg, seg, rope_tabs, kv_dtype):
    n = x.shape[0]
    tiles_per_seq = seq_len // tm if seq_len >= tm else 1
    rope = rope_tabs is not None
    row = lambda i: (i, 0)
    in_specs = [
        pl.BlockSpec((tm, D_MODEL), row),
        pl.BlockSpec((1, 1, D_MODEL), lambda i: (ada_row(i), 0, 0)),
        pl.BlockSpec((1, 1, D_MODEL), lambda i: (ada_row(i), 0, 1)),
        _resident((1, D_MODEL)),
        _resident((D_MODEL, IN_WIDTH)),
        _resident((1, MXU_N)),
        _resident((1, MXU_N)),
        _resident((MXU_N, MXU_N)),
    ]
    args = [x, ada3, ada3, g_pre1, w_in, qg, kg, seg]
    if rope:
        in_specs += [pl.BlockSpec((tm, LANES), lambda i: (i % tiles_per_seq, 0))] * 2
        args += list(rope_tabs)
    wide = jax.ShapeDtypeStruct((n, D_MODEL), BF16)
    kv = jax.ShapeDtypeStruct((n, KV_WIDTH), kv_dtype)
    return pl.pallas_call(
        functools.partial(_in_proj_kernel, rope),
        grid=(n // tm,),
        in_specs=in_specs,
        out_specs=[pl.BlockSpec((tm, D_MODEL), row), pl.BlockSpec((tm, KV_WIDTH), row), pl.BlockSpec((tm, KV_WIDTH), row)]
        + [pl.BlockSpec((tm, D_MODEL), row)] * 4,
        out_shape=[wide, kv, kv, wide, wide, wide, wide],
        compiler_params=_params(1),
        name="in_proj_rope" if rope else "in_proj",
    )(*args)


def _scores(q_ref, kt_ref, col0, g):
    lo = (g // 2) * LANES
    q_pair = q_ref[0, :, col0 + lo:col0 + lo + LANES]
    slot = jax.lax.broadcasted_iota(jnp.int32, (1, LANES), 1) // HEAD_DIM
    q_one = jnp.where(slot == g % 2, q_pair, jnp.zeros_like(q_pair))
    return _dot(q_one, kt_ref[0, lo:lo + LANES, :])


def _group_of_lane():
    return jax.lax.broadcasted_iota(jnp.int32, (1, MXU_N), 1) // HEAD_DIM


def _sum_lane(g):
    return (HEAD_DIM * (g + 1)) % MXU_N


def _build_masked_v(v_ref, vm_ref):
    lane = jax.lax.broadcasted_iota(jnp.int32, (1, MXU_N), 1)
    v = v_ref[0].astype(F32)
    for g in range(N_KV_HEADS):
        ones_col = (lane == _sum_lane(g)).astype(F32)
        vm_ref[g] = jnp.where(_group_of_lane() == g, v, ones_col).astype(BF16)


def _unnormalised_probs(s):
    return jnp.exp2(s - jnp.max(s, axis=-1, keepdims=True)).astype(BF16)


def _weighted_values(probs, vm_ref):
    acc = None
    den = None
    for g in range(N_KV_HEADS):
        pv = _dot(probs[g], vm_ref[g])
        own = _group_of_lane() == g
        row_sum = pv[:, _sum_lane(g):_sum_lane(g) + 1]
        acc = pv if acc is None else jnp.where(own, pv, acc)
        den = row_sum if den is None else jnp.where(own, row_sum, den)
    return acc / den


def _attn_kernel(n_sets, q_ref, kt_ref, v_ref, o_ref, vm_ref):
    @pl.when((pl.program_id(1) == 0) & (pl.program_id(2) == 0))
    def _():
        _build_masked_v(v_ref, vm_ref)

    for j in range(n_sets):
        probs = [_unnormalised_probs(_scores(q_ref, kt_ref, j * MXU_N, g)) for g in range(N_KV_HEADS)]
        o_ref[0, :, j * MXU_N:(j + 1) * MXU_N] = _weighted_values(probs, vm_ref).astype(BF16)


def _attention(q, kt, v, tq, sets_per_step):
    bsz, n, _ = q.shape
    t = kt.shape[-1]
    width = sets_per_step * MXU_N
    return pl.pallas_call(
        functools.partial(_attn_kernel, sets_per_step),
        grid=(bsz, GROUP // sets_per_step, n // tq),
        in_specs=[
            pl.BlockSpec((1, tq, width), lambda b, j, i: (b, i, j)),
            pl.BlockSpec((1, KV_WIDTH, t), lambda b, j, i: (b, 0, 0)),
            pl.BlockSpec((1, t, KV_WIDTH), lambda b, j, i: (b, 0, 0)),
        ],
        out_specs=pl.BlockSpec((1, tq, width), lambda b, j, i: (b, i, j)),
        out_shape=jax.ShapeDtypeStruct(q.shape, BF16),
        scratch_shapes=[pltpu.VMEM((N_KV_HEADS, t, MXU_N), BF16)],
        compiler_params=_params(3),
        name=f"attn_t{t}",
    )(q, kt, v)


def _attn_pipe_kernel(steps_per_batch, q_ref, kt_ref, v_ref, o_ref, vm_ref, pa_ref, pb_ref, s_ref):
    t = pl.program_id(0)

    @pl.when(t == 0)
    def _():
        pb_ref[...] = jnp.ones_like(pb_ref)

    @pl.when((t == 0) | ((t - 1) % steps_per_batch == 0))
    def _():
        _build_masked_v(v_ref, vm_ref)

    n_chunks = kt_ref.shape[-1] // KEY_CHUNK
    slot_of_lane = jax.lax.broadcasted_iota(jnp.int32, (1, LANES), 1) // HEAD_DIM

    def keys(c):
        return slice(c * KEY_CHUNK, (c + 1) * KEY_CHUNK)

    def score_chunk(g, c, run_max):
        lo = (g // 2) * LANES
        q_pair = q_ref[0, :, lo:lo + LANES]
        q_one = jnp.where(slot_of_lane == g % 2, q_pair, jnp.zeros_like(q_pair))
        s = _dot(q_one, kt_ref[0, lo:lo + LANES, keys(c)])
        s_ref[g % 2, :, keys(c)] = s
        for k in range(KEY_CHUNK // LANES):
            part = s[:, k * LANES:(k + 1) * LANES]
            run_max = part if run_max is None else jnp.maximum(run_max, part)
        return run_max

    def step(p_new, p_old):
        run_max = None
        for c in range(n_chunks):
            run_max = score_chunk(0, c, run_max)
        out = None
        den = None
        for g in range(N_KV_HEADS):
            row_max = jnp.max(run_max, axis=-1, keepdims=True)
            run_max = None
            acc = None
            for c in range(n_chunks):
                p_new[g, :, keys(c)] = jnp.exp2(s_ref[g % 2, :, keys(c)] - row_max).astype(BF16)
                if g + 1 < N_KV_HEADS:
                    run_max = score_chunk(g + 1, c, run_max)
                pv = _dot(p_old[g, :, keys(c)], vm_ref[g, keys(c), :])
                acc = pv if acc is None else acc + pv
            own = _group_of_lane() == g
            row_sum = acc[:, _sum_lane(g):_sum_lane(g) + 1]
            out = acc if out is None else jnp.where(own, acc, out)
            den = row_sum if den is None else jnp.where(own, row_sum, den)
        o_ref[0] = (out / den).astype(BF16)

    @pl.when(t % 2 == 0)
    def _():
        step(pa_ref, pb_ref)

    @pl.when(t % 2 == 1)
    def _():
        step(pb_ref, pa_ref)


def _attention_pipelined(q, kt, v, tq):
    bsz, n, _ = q.shape
    t_keys = kt.shape[-1]
    tiles = n // tq
    steps_per_batch = GROUP * tiles
    n_items = bsz * steps_per_batch

    def item(t):
        return t // steps_per_batch, (t // tiles) % GROUP, t % tiles

    def cur(t):
        return item(jnp.minimum(t, n_items - 1))

    def prev(t):
        return item(jnp.maximum(t - 1, 0))

    def q_map(t):
        b, j, i = cur(t)
        return b, i, j

    def o_map(t):
        b, j, i = prev(t)
        return b, i, j

    return pl.pallas_call(
        functools.partial(_attn_pipe_kernel, steps_per_batch),
        grid=(n_items + 1,),
        in_specs=[
            pl.BlockSpec((1, tq, MXU_N), q_map),
            pl.BlockSpec((1, KV_WIDTH, t_keys), lambda t: (cur(t)[0], 0, 0)),
            pl.BlockSpec((1, t_keys, KV_WIDTH), lambda t: (prev(t)[0], 0, 0)),
        ],
        out_specs=pl.BlockSpec((1, tq, MXU_N), o_map),
        out_shape=jax.ShapeDtypeStruct(q.shape, BF16),
        scratch_shapes=[pltpu.VMEM((N_KV_HEADS, t_keys, MXU_N), BF16),
                        pltpu.VMEM((N_KV_HEADS, tq, t_keys), BF16), pltpu.VMEM((N_KV_HEADS, tq, t_keys), BF16),
                        pltpu.VMEM((2, tq, t_keys), F32)],
        compiler_params=_params(1),
        name=f"attn_pipe_t{t_keys}",
    )(q, kt, v)


def _shifted_rows(x, prev_row, next_row):
    tm = x.shape[0]
    t = jax.lax.broadcasted_iota(jnp.int32, (tm, 1), 0)
    before = jnp.where(t == 0, prev_row, pltpu.roll(x, 1, axis=0))
    after = jnp.where(t == tm - 1, next_row, pltpu.roll(x, tm - 1, axis=0))
    return before, after


def _mix_out_kernel(tiles_per_seq, att_ref, b_ref, y_ref, yp_ref, yn_ref, sa_ref, sg_ref, h_ref, g1_ref, sh2_ref,
                    sc2_ref, gpost_ref, gpre_ref, cw_ref, wa_ref, wc_ref, wo_ref, h1_ref, u2_ref):
    i = pl.program_id(0)
    has_prev = (i % tiles_per_seq != 0).astype(F32)
    has_next = (i % tiles_per_seq != tiles_per_seq - 1).astype(F32)
    y = y_ref[...].astype(F32)
    y_prev = yp_ref[HALO - 1:HALO, :].astype(F32) * has_prev
    y_next = yn_ref[0:1, :].astype(F32) * has_next
    before, after = _shifted_rows(y, y_prev, y_next)
    conv = cw_ref[0:1, :] * before + cw_ref[1:2, :] * y + cw_ref[2:3, :] * after
    conv_in = (b_ref[...].astype(F32) * conv).astype(BF16)

    att = _dot(att_ref[...], wa_ref[...])
    cnv = _dot(conv_in, wc_ref[...])
    merged = (sa_ref[...].astype(F32) * att + sg_ref[...].astype(F32) * cnv).astype(BF16)
    mo = _dot(merged, wo_ref[...])

    h1 = h_ref[...] + g1_ref[0] * (_rms_rows(mo) * gpost_ref[...])
    h1_ref[...] = h1
    u2 = _rms_rows(h1) * (gpre_ref[...] * (1.0 + sc2_ref[0])) + sh2_ref[0]
    u2_ref[...] = u2.astype(BF16)


def _halo_specs(tm, n_rows, width):
    per = tm // HALO
    last = n_rows // HALO - 1
    prev = pl.BlockSpec((HALO, width), lambda i: (jnp.maximum(i * per - 1, 0), 0))
    nxt = pl.BlockSpec((HALO, width), lambda i: (jnp.minimum((i + 1) * per, last), 0))
    return prev, nxt


def _mix_out(att, b, y, sa, sg, h, ada3, ada_row, seq_len, tm, g_post1, g_pre2, conv_w, w_att_out, w_conv_out, w_o):
    n = h.shape[0]
    assert seq_len % tm == 0
    row = lambda i: (i, 0)
    wide = pl.BlockSpec((tm, D_MODEL), row)
    prev, nxt = _halo_specs(tm, n, D_MODEL)
    ada = lambda k: pl.BlockSpec((1, 1, D_MODEL), lambda i: (ada_row(i), 0, k))
    sq = (D_MODEL, D_MODEL)
    return pl.pallas_call(
        functools.partial(_mix_out_kernel, seq_len // tm),
        grid=(n // tm,),
        in_specs=[wide, wide, wide, prev, nxt, wide, wide, wide, ada(2), ada(3), ada(4),
                  _resident((1, D_MODEL)), _resident((1, D_MODEL)), _resident((3, D_MODEL)),
                  _resident(sq), _resident(sq), _resident(sq)],
        out_specs=[wide, wide],
        out_shape=[jax.ShapeDtypeStruct((n, D_MODEL), F32), jax.ShapeDtypeStruct((n, D_MODEL), BF16)],
        compiler_params=_params(1),
        name=f"mix_out_s{seq_len}",
    )(att, b, y, y, y, sa, sg, h, ada3, ada3, ada3, g_post1, g_pre2, conv_w, w_att_out, w_conv_out, w_o)


FF_CHUNK = D_FF // 2


def _ffn_kernel(tiles_per_seq, u_ref, up_ref, un_ref, h_ref, g2_ref, gpost_ref, cw_ref, wup_ref, wdn_ref, o_ref, ext_ref):
    i = pl.program_id(0)
    tm = u_ref.shape[0]
    has_prev = i % tiles_per_seq != 0
    has_next = i % tiles_per_seq != tiles_per_seq - 1
    zeros = jnp.zeros((HALO, D_MODEL), BF16)
    ext_ref[0:HALO, :] = jnp.where(has_prev, up_ref[...], zeros)
    ext_ref[HALO:HALO + tm, :] = u_ref[...]
    ext_ref[HALO + tm:, :] = jnp.where(has_next, un_ref[...], zeros)
    ext = ext_ref[...]
    rows = tm + 2 * HALO

    def conv_up(col):
        z = _dot(ext, wup_ref[:, col:col + FF_CHUNK])
        w = cw_ref[:, col:col + FF_CHUNK]
        before = pltpu.roll(z, 1, axis=0)[HALO:HALO + tm]
        after = pltpu.roll(z, rows - 1, axis=0)[HALO:HALO + tm]
        return w[0:1] * before + w[1:2] * z[HALO:HALO + tm] + w[2:3] * after

    ff = None
    for c in range(D_FF // FF_CHUNK):
        gate = conv_up(c * FF_CHUNK)
        val = conv_up(D_FF + c * FF_CHUNK)
        act = (gate * jax.nn.sigmoid(gate) * val).astype(BF16)
        part = _dot(act, wdn_ref[c * FF_CHUNK:(c + 1) * FF_CHUNK, :])
        ff = part if ff is None else ff + part

    o_ref[...] = h_ref[...] + g2_ref[0] * (_rms_rows(ff) * gpost_ref[...])


def _ffn(u2, h1, ada3, ada_row, seq_len, tm, g_post2, conv_ffn, w_up, w_down):
    n = h1.shape[0]
    assert seq_len % tm == 0
    row = lambda i: (i, 0)
    wide = pl.BlockSpec((tm, D_MODEL), row)
    prev, nxt = _halo_specs(tm, n, D_MODEL)
    return pl.pallas_call(
        functools.partial(_ffn_kernel, seq_len // tm),
        grid=(n // tm,),
        in_specs=[wide, prev, nxt, wide, pl.BlockSpec((1, 1, D_MODEL), lambda i: (ada_row(i), 0, 5)),
                  _resident((1, D_MODEL)), _resident((3, 2 * D_FF)),
                  _resident((D_MODEL, 2 * D_FF)), _resident((D_FF, D_MODEL))],
        out_specs=wide,
        out_shape=jax.ShapeDtypeStruct((n, D_MODEL), F32),
        scratch_shapes=[pltpu.VMEM((tm + 2 * HALO, D_MODEL), BF16)],
        compiler_params=_params(1),
        name=f"ffn_s{seq_len}",
    )(u2, u2, u2, h1, ada3, g_post2, conv_ffn, w_up, w_down)


def _rope_tables(n):
    pos = np.arange(n)
    inv = np.power(ROPE_THETA, -np.arange(0, AXIS_DIM, 2, dtype=np.float64) / AXIS_DIM)
    ang_r = (pos // GRID_W)[:, None] * inv[None, :]
    ang_c = (pos % GRID_W)[:, None] * inv[None, :]
    ang = np.concatenate([ang_r, ang_r, ang_c, ang_c], axis=1)
    sign = np.tile(np.concatenate([-np.ones(AXIS_DIM // 2), np.ones(AXIS_DIM // 2)]), 2)
    reps = LANES // HEAD_DIM
    cos = np.tile(np.cos(ang), (1, reps)).astype(np.float32)
    sin = np.tile(np.sin(ang) * sign[None, :], (1, reps)).astype(np.float32)
    return jnp.asarray(cos), jnp.asarray(sin)


def _segment_mean_matrix():
    head = np.arange(MXU_N) // HEAD_DIM
    return jnp.asarray((head[:, None] == head[None, :]).astype(np.float32) / HEAD_DIM, dtype=BF16)


def _layer(x, ada3, ada_row, seq_len, tiles, w, rope_tabs, cache_kv):
    bsz = x.shape[0]
    n = bsz * seq_len
    xf = x.reshape(n, D_MODEL)
    tm_in, tq, gs, tm_mix, tm_ffn = tiles
    kv_dtype = BF16 if cache_kv is not None else F32
    q, k, v, b, y, sa, sg = _in_proj(xf, ada3, lambda i: ada_row(i, tm_in), seq_len, tm_in, w["w_in"], w["g_pre1"],
                                     w["qg"], w["kg"], w["seg"], rope_tabs, kv_dtype)
    k3 = k.reshape(bsz, seq_len, KV_WIDTH).astype(BF16)
    v3 = v.reshape(bsz, seq_len, KV_WIDTH).astype(BF16)
    if cache_kv is not None:
        ck, cv = cache_kv
        k3 = jnp.concatenate([ck.reshape(bsz, -1, KV_WIDTH).astype(BF16), k3], axis=1)
        v3 = jnp.concatenate([cv.reshape(bsz, -1, KV_WIDTH).astype(BF16), v3], axis=1)
    t = k3.shape[1]
    kt = k3.transpose(0, 2, 1)
    q3 = q.reshape(bsz, seq_len, ATT_WIDTH)
    att = _attention_pipelined(q3, kt, v3, tq) if gs is None else _attention(q3, kt, v3, tq, gs)
    att = att.reshape(n, ATT_WIDTH)
    h1, u2 = _mix_out(att, b, y, sa, sg, xf, ada3, lambda i: ada_row(i, tm_mix), seq_len, tm_mix, w["g_post1"],
                      w["g_pre2"], w["conv_w"], w["w_att_out"], w["w_conv_out"], w["w_o"])
    out = _ffn(u2, h1, ada3, lambda i: ada_row(i, tm_ffn), seq_len, tm_ffn, w["g_post2"], w["conv_ffn"], w["w_up"],
               w["w_down"])
    return out.reshape(x.shape), k, v


def kernel(x_prompt, x_sample, cache_k, cache_v, c, c_ctx, w_ada, b_ada, g_pre1, g_post1, g_pre2, g_post2, w_in, q_norm,
           k_norm, w_att_out, conv_w, w_conv_out, w_o, w_up, conv_ffn, w_down):
    depth = w_in.shape[0]
    dec_batch, dec_seq, _ = x_sample.shape
    batch, seq, _ = x_prompt.shape
    rope_tabs = _rope_tables(dec_seq)
    seg = _segment_mean_matrix()
    cc = jnp.zeros((ADA_ROWS, D_MODEL), F32).at[0].set(c_ctx).at[1:1 + dec_batch].set(c)
    reps = MXU_N // HEAD_DIM

    h_p, h_s = x_prompt, x_sample
    new_ks, new_vs = [], []
    for i in range(depth):
        wq = w_in[i][:, :ATT_WIDTH].reshape(D_MODEL, N_KV_HEADS, GROUP, HEAD_DIM).transpose(0, 2, 1, 3)
        wa = w_att_out[i].reshape(N_KV_HEADS, GROUP, HEAD_DIM, D_MODEL).transpose(1, 0, 2, 3)
        w = {
            "g_pre1": g_pre1[i][None], "g_post1": g_post1[i][None], "g_pre2": g_pre2[i][None], "g_post2": g_post2[i][None],
            "w_in": jnp.concatenate([wq.reshape(D_MODEL, ATT_WIDTH), w_in[i][:, ATT_WIDTH:]], axis=1).astype(BF16),
            "qg": jnp.tile(q_norm[i] * (HEAD_DIM ** -0.5 * LOG2E), reps)[None],
            "kg": jnp.tile(k_norm[i], reps)[None],
            "seg": seg,
            "w_att_out": wa.reshape(ATT_WIDTH, D_MODEL).astype(BF16), "conv_w": conv_w[i],
            "w_conv_out": w_conv_out[i].astype(BF16),
            "w_o": w_o[i].astype(BF16), "w_up": w_up[i].astype(BF16), "conv_ffn": conv_ffn[i],
            "w_down": w_down[i].astype(BF16),
        }
        ada3 = _ada(cc, w_ada[i], b_ada[i]).reshape(ADA_ROWS, N_ADA, D_MODEL).reshape(ADA_ROWS, 1, N_ADA * D_MODEL)
        h_p, k_ctx, v_ctx = _layer(h_p, ada3, lambda t, tm: 0, seq, (512, seq, GROUP, seq, seq), w, None, None)
        new_ks.append(k_ctx.reshape(batch, seq, N_KV_HEADS, HEAD_DIM))
        new_vs.append(v_ctx.reshape(batch, seq, N_KV_HEADS, HEAD_DIM))
        h_s, _, _ = _layer(h_s, ada3, lambda t, tm: 1 + t // (dec_seq // tm), dec_seq, (512, 256, None, 512, 512), w,
                           rope_tabs, (cache_k[:, i], cache_v[:, i]))
    return (h_p, h_s, jnp.stack(new_ks, axis=1), jnp.stack(new_vs, axis=1))
```

```python
import functools

import numpy as np
import jax
import jax.numpy as jnp
from jax.experimental import pallas as pl
from jax.experimental.pallas import tpu as pltpu

D_MODEL = 1024
N_HEADS = 16
N_KV_HEADS = 4
HEAD_DIM = 64
GROUP = N_HEADS // N_KV_HEADS
ATT_WIDTH = N_HEADS * HEAD_DIM
KV_WIDTH = N_KV_HEADS * HEAD_DIM
D_FF = 2816
GRID_W = 64
ROPE_THETA = 10000.0
AXIS_DIM = HEAD_DIM // 2
N_ADA = 6
EPS = 1e-6
LOG2E = 1.4426950408889634

OFF_Q = 0
OFF_K = OFF_Q + ATT_WIDTH
OFF_V = OFF_K + KV_WIDTH
OFF_B = OFF_V + KV_WIDTH
OFF_C = OFF_B + D_MODEL
OFF_X = OFF_C + D_MODEL
OFF_GA = OFF_X + D_MODEL
OFF_GC = OFF_GA + D_MODEL
IN_WIDTH = OFF_GC + D_MODEL

LANES = 128
MXU_N = 256
HALO = 16
ADA_ROWS = 16
VMEM_LIMIT = 56 * 1024 * 1024
KEY_CHUNK = 512

BF16 = jnp.bfloat16
F32 = jnp.float32


def _dot(a, b):
    return jnp.dot(a, b, preferred_element_type=F32)


def _resident(shape):
    nd = len(shape)
    return pl.BlockSpec(shape, lambda *_: (0,) * nd, pipeline_mode=pl.Buffered(1))


def _params(n_axes):
    return pltpu.CompilerParams(dimension_semantics=("arbitrary",) * n_axes, vmem_limit_bytes=VMEM_LIMIT)


def _ada_kernel(c_ref, w_ref, b_ref, o_ref):
    c = c_ref[...]
    s = (c * jax.nn.sigmoid(c)).astype(BF16)
    o_ref[...] = _dot(s, w_ref[...].astype(BF16)) + b_ref[...]


def _ada(cc, w_ada, b_ada):
    n = w_ada.shape[1]
    tn = D_MODEL
    return pl.pallas_call(
        _ada_kernel,
        grid=(n // tn,),
        in_specs=[
            pl.BlockSpec((ADA_ROWS, D_MODEL), lambda j: (0, 0)),
            pl.BlockSpec((D_MODEL, tn), lambda j: (0, j)),
            pl.BlockSpec((1, tn), lambda j: (0, j)),
        ],
        out_specs=pl.BlockSpec((ADA_ROWS, tn), lambda j: (0, j)),
        out_shape=jax.ShapeDtypeStruct((ADA_ROWS, n), F32),
        compiler_params=_params(1),
        name="ada",
    )(cc, w_ada, b_ada.reshape(1, n))


def _rms_rows(x):
    return x * jax.lax.rsqrt(jnp.mean(x * x, axis=-1, keepdims=True) + EPS)


def _head_norm(z, seg_ref, gain):
    ms = _dot((z * z).astype(BF16), seg_ref[...])
    return z * jax.lax.rsqrt(ms + EPS) * gain


def _rope(z, cos, sin, first_half):
    partner = jnp.where(first_half, pltpu.roll(z, LANES - AXIS_DIM // 2, axis=1), pltpu.roll(z, AXIS_DIM // 2, axis=1))
    return z * cos + partner * sin


def _in_proj_kernel(rope, x_ref, sh_ref, sc_ref, gpre_ref, w_ref, qg_ref, kg_ref, seg_ref, *rest):
    if rope:
        cos_ref, sin_ref, q_ref, k_ref, v_ref, b_ref, y_ref, sa_ref, sg_ref = rest
        cos = cos_ref[...]
        sin = sin_ref[...]
        lane = jax.lax.broadcasted_iota(jnp.int32, (1, LANES), 1)
        first_half = (lane % AXIS_DIM) < (AXIS_DIM // 2)
    else:
        q_ref, k_ref, v_ref, b_ref, y_ref, sa_ref, sg_ref = rest

    x = x_ref[...]
    mod = gpre_ref[...] * (1.0 + sc_ref[0])
    u = (_rms_rows(x) * mod + sh_ref[0]).astype(BF16)

    def rotate(z):
        if not rope:
            return z
        return jnp.concatenate(
            [_rope(z[:, h * LANES:(h + 1) * LANES], cos, sin, first_half) for h in range(MXU_N // LANES)], axis=1)

    def cols(j):
        return slice(j * MXU_N, (j + 1) * MXU_N)

    def store_q(j, z):
        q_ref[:, cols(j)] = rotate(_head_norm(z, seg_ref, qg_ref[...])).astype(BF16)

    def store_k(z):
        k_ref[...] = rotate(_head_norm(z, seg_ref, kg_ref[...])).astype(k_ref.dtype)

    def store_v(z):
        v_ref[...] = z.astype(v_ref.dtype)

    def store_b(j, z):
        b_ref[:, cols(j)] = z.astype(BF16)

    def store_y(j, zc, zx):
        y_ref[:, cols(j)] = (zc * zx).astype(BF16)

    def store_gate(ref, j, z):
        ref[:, cols(j)] = jax.nn.sigmoid(z).astype(BF16)

    work = [((OFF_Q + j * MXU_N,), functools.partial(store_q, j)) for j in range(ATT_WIDTH // MXU_N)]
    work += [((OFF_K,), store_k), ((OFF_V,), store_v)]
    for j in range(D_MODEL // MXU_N):
        work += [((OFF_B + j * MXU_N,), functools.partial(store_b, j)),
                 ((OFF_C + j * MXU_N, OFF_X + j * MXU_N), functools.partial(store_y, j)),
                 ((OFF_GA + j * MXU_N,), functools.partial(store_gate, sa_ref, j)),
                 ((OFF_GC + j * MXU_N,), functools.partial(store_gate, sg_ref, j))]
    pending = None
    for offsets, consume in work:
        products = [_dot(u, w_ref[:, lo:lo + MXU_N]) for lo in offsets]
        if pending is not None:
            pending[0](*pending[1])
        pending = (consume, products)
    pending[0](*pending[1])


def _in_proj(x, ada3, ada_row, seq_len, tm, w_in, g_pre1, qg, kg, seg, rope_tabs, kv_dtype):
    n = x.shape[0]
    tiles_per_seq = seq_len // tm if seq_len >= tm else 1
    rope = rope_tabs is not None
    row = lambda i: (i, 0)
    in_specs = [
        pl.BlockSpec((tm, D_MODEL), row),
        pl.BlockSpec((1, 1, D_MODEL), lambda i: (ada_row(i), 0, 0)),
        pl.BlockSpec((1, 1, D_MODEL), lambda i: (ada_row(i), 0, 1)),
        _resident((1, D_MODEL)),
        _resident((D_MODEL, IN_WIDTH)),
        _resident((1, MXU_N)),
        _resident((1, MXU_N)),
        _resident((MXU_N, MXU_N)),
    ]
    args = [x, ada3, ada3, g_pre1, w_in, qg, kg, seg]
    if rope:
        in_specs += [pl.BlockSpec((tm, LANES), lambda i: (i % tiles_per_seq, 0))] * 2
        args += list(rope_tabs)
    wide = jax.ShapeDtypeStruct((n, D_MODEL), BF16)
    kv = jax.ShapeDtypeStruct((n, KV_WIDTH), kv_dtype)
    return pl.pallas_call(
        functools.partial(_in_proj_kernel, rope),
        grid=(n // tm,),
        in_specs=in_specs,
        out_specs=[pl.BlockSpec((tm, D_MODEL), row), pl.BlockSpec((tm, KV_WIDTH), row), pl.BlockSpec((tm, KV_WIDTH), row)]
        + [pl.BlockSpec((tm, D_MODEL), row)] * 4,
        out_shape=[wide, kv, kv, wide, wide, wide, wide],
        compiler_params=_params(1),
        name="in_proj_rope" if rope else "in_proj",
    )(*args)


def _scores(q_ref, kt_ref, col0, g):
    lo = (g // 2) * LANES
    q_pair = q_ref[0, :, col0 + lo:col0 + lo + LANES]
    slot = jax.lax.broadcasted_iota(jnp.int32, (1, LANES), 1) // HEAD_DIM
    q_one = jnp.where(slot == g % 2, q_pair, jnp.zeros_like(q_pair))
    return _dot(q_one, kt_ref[0, lo:lo + LANES, :])


def _group_of_lane():
    return jax.lax.broadcasted_iota(jnp.int32, (1, MXU_N), 1) // HEAD_DIM


def _sum_lane(g):
    return (HEAD_DIM * (g + 1)) % MXU_N


def _build_masked_v(v_ref, vm_ref):
    lane = jax.lax.broadcasted_iota(jnp.int32, (1, MXU_N), 1)
    v = v_ref[0].astype(F32)
    for g in range(N_KV_HEADS):
        ones_col = (lane == _sum_lane(g)).astype(F32)
        vm_ref[g] = jnp.where(_group_of_lane() == g, v, ones_col).astype(BF16)


def _unnormalised_probs(s):
    return jnp.exp2(s - jnp.max(s, axis=-1, keepdims=True)).astype(BF16)


def _weighted_values(probs, vm_ref):
    acc = None
    den = None
    for g in range(N_KV_HEADS):
        pv = _dot(probs[g], vm_ref[g])
        own = _group_of_lane() == g
        row_sum = pv[:, _sum_lane(g):_sum_lane(g) + 1]
        acc = pv if acc is None else jnp.where(own, pv, acc)
        den = row_sum if den is None else jnp.where(own, row_sum, den)
    return acc / den


def _attn_kernel(n_sets, q_ref, kt_ref, v_ref, o_ref, vm_ref):
    @pl.when((pl.program_id(1) == 0) & (pl.program_id(2) == 0))
    def _():
        _build_masked_v(v_ref, vm_ref)

    for j in range(n_sets):
        probs = [_unnormalised_probs(_scores(q_ref, kt_ref, j * MXU_N, g)) for g in range(N_KV_HEADS)]
        o_ref[0, :, j * MXU_N:(j + 1) * MXU_N] = _weighted_values(probs, vm_ref).astype(BF16)


def _attention(q, kt, v, tq, sets_per_step):
    bsz, n, _ = q.shape
    t = kt.shape[-1]
    width = sets_per_step * MXU_N
    return pl.pallas_call(
        functools.partial(_attn_kernel, sets_per_step),
        grid=(bsz, GROUP // sets_per_step, n // tq),
        in_specs=[
            pl.BlockSpec((1, tq, width), lambda b, j, i: (b, i, j)),
            pl.BlockSpec((1, KV_WIDTH, t), lambda b, j, i: (b, 0, 0)),
            pl.BlockSpec((1, t, KV_WIDTH), lambda b, j, i: (b, 0, 0)),
        ],
        out_specs=pl.BlockSpec((1, tq, width), lambda b, j, i: (b, i, j)),
        out_shape=jax.ShapeDtypeStruct(q.shape, BF16),
        scratch_shapes=[pltpu.VMEM((N_KV_HEADS, t, MXU_N), BF16)],
        compiler_params=_params(3),
        name=f"attn_t{t}",
    )(q, kt, v)


def _attn_pipe_kernel(steps_per_batch, q_ref, kt_ref, v_ref, o_ref, vm_ref, pa_ref, pb_ref, s_ref):
    t = pl.program_id(0)

    @pl.when(t == 0)
    def _():
        pb_ref[...] = jnp.ones_like(pb_ref)

    @pl.when((t == 0) | ((t - 1) % steps_per_batch == 0))
    def _():
        _build_masked_v(v_ref, vm_ref)

    n_chunks = kt_ref.shape[-1] // KEY_CHUNK
    slot_of_lane = jax.lax.broadcasted_iota(jnp.int32, (1, LANES), 1) // HEAD_DIM

    def keys(c):
        return slice(c * KEY_CHUNK, (c + 1) * KEY_CHUNK)

    def score_chunk(g, c, run_max):
        lo = (g // 2) * LANES
        q_pair = q_ref[0, :, lo:lo + LANES]
        q_one = jnp.where(slot_of_lane == g % 2, q_pair, jnp.zeros_like(q_pair))
        s = _dot(q_one, kt_ref[0, lo:lo + LANES, keys(c)])
        s_ref[g % 2, :, keys(c)] = s
        for k in range(KEY_CHUNK // LANES):
            part = s[:, k * LANES:(k + 1) * LANES]
            run_max = part if run_max is None else jnp.maximum(run_max, part)
        return run_max

    def step(p_new, p_old):
        run_max = None
        for c in range(n_chunks):
            run_max = score_chunk(0, c, run_max)
        out = None
        den = None
        for g in range(N_KV_HEADS):
            row_max = jnp.max(run_max, axis=-1, keepdims=True)
            run_max = None
            acc = None
            for c in range(n_chunks):
                p_new[g, :, keys(c)] = jnp.exp2(s_ref[g % 2, :, keys(c)] - row_max).astype(BF16)
                if g + 1 < N_KV_HEADS:
                    run_max = score_chunk(g + 1, c, run_max)
                pv = _dot(p_old[g, :, keys(c)], vm_ref[g, keys(c), :])
                acc = pv if acc is None else acc + pv
            own = _group_of_lane() == g
            row_sum = acc[:, _sum_lane(g):_sum_lane(g) + 1]
            out = acc if out is None else jnp.where(own, acc, out)
            den = row_sum if den is None else jnp.where(own, row_sum, den)
        o_ref[0] = (out / den).astype(BF16)

    @pl.when(t % 2 == 0)
    def _():
        step(pa_ref, pb_ref)

    @pl.when(t % 2 == 1)
    def _():
        step(pb_ref, pa_ref)


def _attention_pipelined(q, kt, v, tq):
    bsz, n, _ = q.shape
    t_keys = kt.shape[-1]
    tiles = n // tq
    steps_per_batch = GROUP * tiles
    n_items = bsz * steps_per_batch

    def item(t):
        return t // steps_per_batch, (t // tiles) % GROUP, t % tiles

    def cur(t):
        return item(jnp.minimum(t, n_items - 1))

    def prev(t):
        return item(jnp.maximum(t - 1, 0))

    def q_map(t):
        b, j, i = cur(t)
        return b, i, j

    def o_map(t):
        b, j, i = prev(t)
        return b, i, j

    return pl.pallas_call(
        functools.partial(_attn_pipe_kernel, steps_per_batch),
        grid=(n_items + 1,),
        in_specs=[
            pl.BlockSpec((1, tq, MXU_N), q_map),
            pl.BlockSpec((1, KV_WIDTH, t_keys), lambda t: (cur(t)[0], 0, 0)),
            pl.BlockSpec((1, t_keys, KV_WIDTH), lambda t: (prev(t)[0], 0, 0)),
        ],
        out_specs=pl.BlockSpec((1, tq, MXU_N), o_map),
        out_shape=jax.ShapeDtypeStruct(q.shape, BF16),
        scratch_shapes=[pltpu.VMEM((N_KV_HEADS, t_keys, MXU_N), BF16),
                        pltpu.VMEM((N_KV_HEADS, tq, t_keys), BF16), pltpu.VMEM((N_KV_HEADS, tq, t_keys), BF16),
                        pltpu.VMEM((2, tq, t_keys), F32)],
        compiler_params=_params(1),
        name=f"attn_pipe_t{t_keys}",
    )(q, kt, v)


def _shifted_rows(x, prev_row, next_row, period):
    tm = x.shape[0]
    t = jax.lax.broadcasted_iota(jnp.int32, (tm, 1), 0) % period
    before = jnp.where(t == 0, prev_row, pltpu.roll(x, 1, axis=0))
    after = jnp.where(t == period - 1, next_row, pltpu.roll(x, tm - 1, axis=0))
    return before, after


def _mix_out_kernel(tiles_per_seq, period, att_ref, b_ref, y_ref, yp_ref, yn_ref, sa_ref, sg_ref, h_ref, g1_ref, sh2_ref,
                    sc2_ref, gpost_ref, gpre_ref, cw_ref, wa_ref, wc_ref, wo_ref, h1_ref, u2_ref):
    i = pl.program_id(0)
    tm = y_ref.shape[0]
    halves = [slice(0, tm // 2), slice(tm // 2, tm)]
    att = [_dot(att_ref[r, :], wa_ref[...]) for r in halves]

    has_prev = (i % tiles_per_seq != 0).astype(F32)
    has_next = (i % tiles_per_seq != tiles_per_seq - 1).astype(F32)
    y = y_ref[...].astype(F32)
    y_prev = yp_ref[HALO - 1:HALO, :].astype(F32) * has_prev
    y_next = yn_ref[0:1, :].astype(F32) * has_next
    before, after = _shifted_rows(y, y_prev, y_next, period)
    conv = cw_ref[0:1, :] * before + cw_ref[1:2, :] * y + cw_ref[2:3, :] * after
    conv_in = (b_ref[...].astype(F32) * conv).astype(BF16)
    cnv = [_dot(conv_in[r, :], wc_ref[...]) for r in halves]

    mixed = []
    for r, a, c in zip(halves, att, cnv):
        merged = (sa_ref[r, :].astype(F32) * a + sg_ref[r, :].astype(F32) * c).astype(BF16)
        mixed.append(_dot(merged, wo_ref[...]))

    for r, mo in zip(halves, mixed):
        h1 = h_ref[r, :] + g1_ref[0] * (_rms_rows(mo) * gpost_ref[...])
        h1_ref[r, :] = h1
        u2 = _rms_rows(h1) * (gpre_ref[...] * (1.0 + sc2_ref[0])) + sh2_ref[0]
        u2_ref[r, :] = u2.astype(BF16)


def _halo_specs(tm, n_rows, width):
    per = tm // HALO
    last = n_rows // HALO - 1
    prev = pl.BlockSpec((HALO, width), lambda i: (jnp.maximum(i * per - 1, 0), 0))
    nxt = pl.BlockSpec((HALO, width), lambda i: (jnp.minimum((i + 1) * per, last), 0))
    return prev, nxt


def _mix_out(att, b, y, sa, sg, h, ada3, ada_row, seq_len, tm, g_post1, g_pre2, conv_w, w_att_out, w_conv_out, w_o):
    n = h.shape[0]
    assert seq_len % tm == 0 or tm % seq_len == 0
    tiles_per_seq = max(seq_len // tm, 1)
    row = lambda i: (i, 0)
    wide = pl.BlockSpec((tm, D_MODEL), row)
    prev, nxt = _halo_specs(tm, n, D_MODEL)
    ada = lambda k: pl.BlockSpec((1, 1, D_MODEL), lambda i: (ada_row(i), 0, k))
    sq = (D_MODEL, D_MODEL)
    return pl.pallas_call(
        functools.partial(_mix_out_kernel, tiles_per_seq, min(seq_len, tm)),
        grid=(n // tm,),
        in_specs=[wide, wide, wide, prev, nxt, wide, wide, wide, ada(2), ada(3), ada(4),
                  _resident((1, D_MODEL)), _resident((1, D_MODEL)), _resident((3, D_MODEL)),
                  _resident(sq), _resident(sq), _resident(sq)],
        out_specs=[wide, wide],
        out_shape=[jax.ShapeDtypeStruct((n, D_MODEL), F32), jax.ShapeDtypeStruct((n, D_MODEL), BF16)],
        compiler_params=_params(1),
        name=f"mix_out_s{seq_len}",
    )(att, b, y, y, y, sa, sg, h, ada3, ada3, ada3, g_post1, g_pre2, conv_w, w_att_out, w_conv_out, w_o)


FF_CHUNK = D_FF // 2


def _ffn_kernel(tiles_per_seq, u_ref, up_ref, un_ref, h_ref, g2_ref, gpost_ref, cw_ref, wup_ref, wdn_ref, o_ref, ext_ref):
    i = pl.program_id(0)
    tm = u_ref.shape[0]
    has_prev = i % tiles_per_seq != 0
    has_next = i % tiles_per_seq != tiles_per_seq - 1
    zeros = jnp.zeros((HALO, D_MODEL), BF16)
    ext_ref[0:HALO, :] = jnp.where(has_prev, up_ref[...], zeros)
    ext_ref[HALO:HALO + tm, :] = u_ref[...]
    ext_ref[HALO + tm:, :] = jnp.where(has_next, un_ref[...], zeros)
    ext = ext_ref[...]
    rows = tm + 2 * HALO

    def conv_up(col):
        z = _dot(ext, wup_ref[:, col:col + FF_CHUNK])
        w = cw_ref[:, col:col + FF_CHUNK]
        before = pltpu.roll(z, 1, axis=0)[HALO:HALO + tm]
        after = pltpu.roll(z, rows - 1, axis=0)[HALO:HALO + tm]
        return w[0:1] * before + w[1:2] * z[HALO:HALO + tm] + w[2:3] * after

    ff = None
    for c in range(D_FF // FF_CHUNK):
        gate = conv_up(c * FF_CHUNK)
        val = conv_up(D_FF + c * FF_CHUNK)
        act = (gate * jax.nn.sigmoid(gate) * val).astype(BF16)
        part = _dot(act, wdn_ref[c * FF_CHUNK:(c + 1) * FF_CHUNK, :])
        ff = part if ff is None else ff + part

    o_ref[...] = h_ref[...] + g2_ref[0] * (_rms_rows(ff) * gpost_ref[...])


def _ffn(u2, h1, ada3, ada_row, seq_len, tm, g_post2, conv_ffn, w_up, w_down):
    n = h1.shape[0]
    assert seq_len % tm == 0
    row = lambda i: (i, 0)
    wide = pl.BlockSpec((tm, D_MODEL), row)
    prev, nxt = _halo_specs(tm, n, D_MODEL)
    return pl.pallas_call(
        functools.partial(_ffn_kernel, seq_len // tm),
        grid=(n // tm,),
        in_specs=[wide, prev, nxt, wide, pl.BlockSpec((1, 1, D_MODEL), lambda i: (ada_row(i), 0, 5)),
                  _resident((1, D_MODEL)), _resident((3, 2 * D_FF)),
                  _resident((D_MODEL, 2 * D_FF)), _resident((D_FF, D_MODEL))],
        out_specs=wide,
        out_shape=jax.ShapeDtypeStruct((n, D_MODEL), F32),
        scratch_shapes=[pltpu.VMEM((tm + 2 * HALO, D_MODEL), BF16)],
        compiler_params=_params(1),
        name=f"ffn_s{seq_len}",
    )(u2, u2, u2, h1, ada3, g_post2, conv_ffn, w_up, w_down)


def _rope_tables(n):
    pos = np.arange(n)
    inv = np.power(ROPE_THETA, -np.arange(0, AXIS_DIM, 2, dtype=np.float64) / AXIS_DIM)
    ang_r = (pos // GRID_W)[:, None] * inv[None, :]
    ang_c = (pos % GRID_W)[:, None] * inv[None, :]
    ang = np.concatenate([ang_r, ang_r, ang_c, ang_c], axis=1)
    sign = np.tile(np.concatenate([-np.ones(AXIS_DIM // 2), np.ones(AXIS_DIM // 2)]), 2)
    reps = LANES // HEAD_DIM
    cos = np.tile(np.cos(ang), (1, reps)).astype(np.float32)
    sin = np.tile(np.sin(ang) * sign[None, :], (1, reps)).astype(np.float32)
    return jnp.asarray(cos), jnp.asarray(sin)


def _segment_mean_matrix():
    head = np.arange(MXU_N) // HEAD_DIM
    return jnp.asarray((head[:, None] == head[None, :]).astype(np.float32) / HEAD_DIM, dtype=BF16)


def _layer(x, ada3, ada_row, seq_len, tiles, w, rope_tabs, cache_kv):
    bsz = x.shape[0]
    n = bsz * seq_len
    xf = x.reshape(n, D_MODEL)
    tm_in, tq, gs, tm_mix, tm_ffn = tiles
    kv_dtype = BF16 if cache_kv is not None else F32
    q, k, v, b, y, sa, sg = _in_proj(xf, ada3, lambda i: ada_row(i, tm_in), seq_len, tm_in, w["w_in"], w["g_pre1"],
                                     w["qg"], w["kg"], w["seg"], rope_tabs, kv_dtype)
    k3 = k.reshape(bsz, seq_len, KV_WIDTH).astype(BF16)
    v3 = v.reshape(bsz, seq_len, KV_WIDTH).astype(BF16)
    if cache_kv is not None:
        ck, cv = cache_kv
        k3 = jnp.concatenate([ck.reshape(bsz, -1, KV_WIDTH).astype(BF16), k3], axis=1)
        v3 = jnp.concatenate([cv.reshape(bsz, -1, KV_WIDTH).astype(BF16), v3], axis=1)
    t = k3.shape[1]
    kt = k3.transpose(0, 2, 1)
    q3 = q.reshape(bsz, seq_len, ATT_WIDTH)
    att = _attention_pipelined(q3, kt, v3, tq) if gs is None else _attention(q3, kt, v3, tq, gs)
    att = att.reshape(n, ATT_WIDTH)
    h1, u2 = _mix_out(att, b, y, sa, sg, xf, ada3, lambda i: ada_row(i, tm_mix), seq_len, tm_mix, w["g_post1"],
                      w["g_pre2"], w["conv_w"], w["w_att_out"], w["w_conv_out"], w["w_o"])
    out = _ffn(u2, h1, ada3, lambda i: ada_row(i, tm_ffn), seq_len, tm_ffn, w["g_post2"], w["conv_ffn"], w["w_up"],
               w["w_down"])
    return out.reshape(x.shape), k, v


def kernel(x_prompt, x_sample, cache_k, cache_v, c, c_ctx, w_ada, b_ada, g_pre1, g_post1, g_pre2, g_post2, w_in, q_norm,
           k_norm, w_att_out, conv_w, w_conv_out, w_o, w_up, conv_ffn, w_down):
    depth = w_in.shape[0]
    dec_batch, dec_seq, _ = x_sample.shape
    batch, seq, _ = x_prompt.shape
    rope_tabs = _rope_tables(dec_seq)
    seg = _segment_mean_matrix()
    cc = jnp.zeros((ADA_ROWS, D_MODEL), F32).at[0].set(c_ctx).at[1:1 + dec_batch].set(c)
    reps = MXU_N // HEAD_DIM

    h_p, h_s = x_prompt, x_sample
    new_ks, new_vs = [], []
    for i in range(depth):
        wq = w_in[i][:, :ATT_WIDTH].reshape(D_MODEL, N_KV_HEADS, GROUP, HEAD_DIM).transpose(0, 2, 1, 3)
        wa = w_att_out[i].reshape(N_KV_HEADS, GROUP, HEAD_DIM, D_MODEL).transpose(1, 0, 2, 3)
        w = {
            "g_pre1": g_pre1[i][None], "g_post1": g_post1[i][None], "g_pre2": g_pre2[i][None], "g_post2": g_post2[i][None],
            "w_in": jnp.concatenate([wq.reshape(D_MODEL, ATT_WIDTH), w_in[i][:, ATT_WIDTH:]], axis=1).astype(BF16),
            "qg": jnp.tile(q_norm[i] * (HEAD_DIM ** -0.5 * LOG2E), reps)[None],
            "kg": jnp.tile(k_norm[i], reps)[None],
            "seg": seg,
            "w_att_out": wa.reshape(ATT_WIDTH, D_MODEL).astype(BF16), "conv_w": conv_w[i],
            "w_conv_out": w_conv_out[i].astype(BF16),
            "w_o": w_o[i].astype(BF16), "w_up": w_up[i].astype(BF16), "conv_ffn": conv_ffn[i],
            "w_down": w_down[i].astype(BF16),
        }
        ada3 = _ada(cc, w_ada[i], b_ada[i]).reshape(ADA_ROWS, N_ADA, D_MODEL).reshape(ADA_ROWS, 1, N_ADA * D_MODEL)
        h_p, k_ctx, v_ctx = _layer(h_p, ada3, lambda t, tm: 0, seq, (512, seq, GROUP, 512, seq), w, None, None)
        new_ks.append(k_ctx.reshape(batch, seq, N_KV_HEADS, HEAD_DIM))
        new_vs.append(v_ctx.reshape(batch, seq, N_KV_HEADS, HEAD_DIM))
        h_s, _, _ = _layer(h_s, ada3, lambda t, tm: 1 + t // (dec_seq // tm), dec_seq, (512, 256, None, 512, 512), w,
                           rope_tabs, (cache_k[:, i], cache_v[:, i]))
    return (h_p, h_s, jnp.stack(new_ks, axis=1), jnp.stack(new_vs, axis=1))
```

```python
import functools

import numpy as np
import jax
import jax.numpy as jnp
from jax.experimental import pallas as pl
from jax.experimental.pallas import tpu as pltpu

D_MODEL = 1024
N_HEADS = 16
N_KV_HEADS = 4
HEAD_DIM = 64
GROUP = N_HEADS // N_KV_HEADS
ATT_WIDTH = N_HEADS * HEAD_DIM
KV_WIDTH = N_KV_HEADS * HEAD_DIM
D_FF = 2816
GRID_W = 64
ROPE_THETA = 10000.0
AXIS_DIM = HEAD_DIM // 2
N_ADA = 6
EPS = 1e-6
LOG2E = 1.4426950408889634

OFF_Q = 0
OFF_K = OFF_Q + ATT_WIDTH
OFF_V = OFF_K + KV_WIDTH
OFF_B = OFF_V + KV_WIDTH
OFF_C = OFF_B + D_MODEL
OFF_X = OFF_C + D_MODEL
OFF_GA = OFF_X + D_MODEL
OFF_GC = OFF_GA + D_MODEL
IN_WIDTH = OFF_GC + D_MODEL

LANES = 128
MXU_N = 256
HALO = 16
ADA_ROWS = 16
VMEM_LIMIT = 56 * 1024 * 1024
KEY_CHUNK = 512

BF16 = jnp.bfloat16
F32 = jnp.float32


def _dot(a, b):
    return jnp.dot(a, b, preferred_element_type=F32)


def _resident(shape):
    nd = len(shape)
    return pl.BlockSpec(shape, lambda *_: (0,) * nd, pipeline_mode=pl.Buffered(1))


def _params(n_axes):
    return pltpu.CompilerParams(dimension_semantics=("arbitrary",) * n_axes, vmem_limit_bytes=VMEM_LIMIT)


def _ada_kernel(c_ref, w_ref, b_ref, o_ref):
    c = c_ref[...]
    s = (c * jax.nn.sigmoid(c)).astype(BF16)
    o_ref[...] = _dot(s, w_ref[...].astype(BF16)) + b_ref[...]


def _ada(cc, w_ada, b_ada):
    n = w_ada.shape[1]
    tn = D_MODEL
    return pl.pallas_call(
        _ada_kernel,
        grid=(n // tn,),
        in_specs=[
            pl.BlockSpec((ADA_ROWS, D_MODEL), lambda j: (0, 0)),
            pl.BlockSpec((D_MODEL, tn), lambda j: (0, j)),
            pl.BlockSpec((1, tn), lambda j: (0, j)),
        ],
        out_specs=pl.BlockSpec((ADA_ROWS, tn), lambda j: (0, j)),
        out_shape=jax.ShapeDtypeStruct((ADA_ROWS, n), F32),
        compiler_params=_params(1),
        name="ada",
    )(cc, w_ada, b_ada.reshape(1, n))


def _rms_rows(x):
    return x * jax.lax.rsqrt(jnp.mean(x * x, axis=-1, keepdims=True) + EPS)


def _head_norm(z, seg_ref, gain):
    ms = _dot((z * z).astype(BF16), seg_ref[...])
    return z * jax.lax.rsqrt(ms + EPS) * gain


def _rope(z, cos, sin, first_half):
    partner = jnp.where(first_half, pltpu.roll(z, LANES - AXIS_DIM // 2, axis=1), pltpu.roll(z, AXIS_DIM // 2, axis=1))
    return z * cos + partner * sin


def _in_proj_kernel(rope, x_ref, sh_ref, sc_ref, gpre_ref, w_ref, qg_ref, kg_ref, seg_ref, *rest):
    if rope:
        cos_ref, sin_ref, q_ref, k_ref, v_ref, b_ref, y_ref, sa_ref, sg_ref = rest
        cos = cos_ref[...]
        sin = sin_ref[...]
        lane = jax.lax.broadcasted_iota(jnp.int32, (1, LANES), 1)
        first_half = (lane % AXIS_DIM) < (AXIS_DIM // 2)
    else:
        q_ref, k_ref, v_ref, b_ref, y_ref, sa_ref, sg_ref = rest

    x = x_ref[...]
    mod = gpre_ref[...] * (1.0 + sc_ref[0])
    u = (_rms_rows(x) * mod + sh_ref[0]).astype(BF16)

    def rotate(z):
        if not rope:
            return z
        return jnp.concatenate(
            [_rope(z[:, h * LANES:(h + 1) * LANES], cos, sin, first_half) for h in range(MXU_N // LANES)], axis=1)

    def cols(j):
        return slice(j * MXU_N, (j + 1) * MXU_N)

    def store_q(j, z):
        q_ref[:, cols(j)] = rotate(_head_norm(z, seg_ref, qg_ref[...])).astype(BF16)

    def store_k(z):
        k_ref[...] = rotate(_head_norm(z, seg_ref, kg_ref[...])).astype(k_ref.dtype)

    def store_v(z):
        v_ref[...] = z.astype(v_ref.dtype)

    def store_b(j, z):
        b_ref[:, cols(j)] = z.astype(BF16)

    def store_y(j, zc, zx):
        y_ref[:, cols(j)] = (zc * zx).astype(BF16)

    def store_gate(ref, j, z):
        ref[:, cols(j)] = jax.nn.sigmoid(z).astype(BF16)

    work = [((OFF_Q + j * MXU_N,), functools.partial(store_q, j)) for j in range(ATT_WIDTH // MXU_N)]
    work += [((OFF_K,), store_k), ((OFF_V,), store_v)]
    for j in range(D_MODEL // MXU_N):
        work += [((OFF_B + j * MXU_N,), functools.partial(store_b, j)),
                 ((OFF_C + j * MXU_N, OFF_X + j * MXU_N), functools.partial(store_y, j)),
                 ((OFF_GA + j * MXU_N,), functools.partial(store_gate, sa_ref, j)),
                 ((OFF_GC + j * MXU_N,), functools.partial(store_gate, sg_ref, j))]
    pending = None
    for offsets, consume in work:
        products = [_dot(u, w_ref[:, lo:lo + MXU_N]) for lo in offsets]
        if pending is not None:
            pending[0](*pending[1])
        pending = (consume, products)
    pending[0](*pending[1])


def _in_proj(x, ada3, ada_row, seq_len, tm, w_in, g_pre1, qg, kg, seg, rope_tabs, kv_dtype):
    n = x.shape[0]
    tiles_per_seq = seq_len // tm if seq_len >= tm else 1
    rope = rope_tabs is not None
    row = lambda i: (i, 0)
    in_specs = [
        pl.BlockSpec((tm, D_MODEL), row),
        pl.BlockSpec((1, 1, D_MODEL), lambda i: (ada_row(i), 0, 0)),
        pl.BlockSpec((1, 1, D_MODEL), lambda i: (ada_row(i), 0, 1)),
        _resident((1, D_MODEL)),
        _resident((D_MODEL, IN_WIDTH)),
        _resident((1, MXU_N)),
        _resident((1, MXU_N)),
        _resident((MXU_N, MXU_N)),
    ]
    args = [x, ada3, ada3, g_pre1, w_in, qg, kg, seg]
    if rope:
        in_specs += [pl.BlockSpec((tm, LANES), lambda i: (i % tiles_per_seq, 0))] * 2
        args += list(rope_tabs)
    wide = jax.ShapeDtypeStruct((n, D_MODEL), BF16)
    kv = jax.ShapeDtypeStruct((n, KV_WIDTH), kv_dtype)
    return pl.pallas_call(
        functools.partial(_in_proj_kernel, rope),
        grid=(n // tm,),
        in_specs=in_specs,
        out_specs=[pl.BlockSpec((tm, D_MODEL), row), pl.BlockSpec((tm, KV_WIDTH), row), pl.BlockSpec((tm, KV_WIDTH), row)]
        + [pl.BlockSpec((tm, D_MODEL), row)] * 4,
        out_shape=[wide, kv, kv, wide, wide, wide, wide],
        compiler_params=_params(1),
        name="in_proj_rope" if rope else "in_proj",
    )(*args)


def _scores(q_ref, kt_ref, col0, g):
    lo = (g // 2) * LANES
    q_pair = q_ref[0, :, col0 + lo:col0 + lo + LANES]
    slot = jax.lax.broadcasted_iota(jnp.int32, (1, LANES), 1) // HEAD_DIM
    q_one = jnp.where(slot == g % 2, q_pair, jnp.zeros_like(q_pair))
    return _dot(q_one, kt_ref[0, lo:lo + LANES, :])


def _group_of_lane():
    return jax.lax.broadcasted_iota(jnp.int32, (1, MXU_N), 1) // HEAD_DIM


def _sum_lane(g):
    return (HEAD_DIM * (g + 1)) % MXU_N


def _build_masked_v(v_ref, vm_ref):
    lane = jax.lax.broadcasted_iota(jnp.int32, (1, MXU_N), 1)
    v = v_ref[0].astype(F32)
    for g in range(N_KV_HEADS):
        ones_col = (lane == _sum_lane(g)).astype(F32)
        vm_ref[g] = jnp.where(_group_of_lane() == g, v, ones_col).astype(BF16)


def _unnormalised_probs(s):
    return jnp.exp2(s - jnp.max(s, axis=-1, keepdims=True)).astype(BF16)


def _weighted_values(probs, vm_ref):
    acc = None
    den = None
    for g in range(N_KV_HEADS):
        pv = _dot(probs[g], vm_ref[g])
        own = _group_of_lane() == g
        row_sum = pv[:, _sum_lane(g):_sum_lane(g) + 1]
        acc = pv if acc is None else jnp.where(own, pv, acc)
        den = row_sum if den is None else jnp.where(own, row_sum, den)
    return acc / den


def _attn_kernel(n_sets, q_ref, kt_ref, v_ref, o_ref, vm_ref):
    @pl.when((pl.program_id(1) == 0) & (pl.program_id(2) == 0))
    def _():
        _build_masked_v(v_ref, vm_ref)

    for j in range(n_sets):
        probs = [_unnormalised_probs(_scores(q_ref, kt_ref, j * MXU_N, g)) for g in range(N_KV_HEADS)]
        o_ref[0, :, j * MXU_N:(j + 1) * MXU_N] = _weighted_values(probs, vm_ref).astype(BF16)


def _attention(q, kt, v, tq, sets_per_step):
    bsz, n, _ = q.shape
    t = kt.shape[-1]
    width = sets_per_step * MXU_N
    return pl.pallas_call(
        functools.partial(_attn_kernel, sets_per_step),
        grid=(bsz, GROUP // sets_per_step, n // tq),
        in_specs=[
            pl.BlockSpec((1, tq, width), lambda b, j, i: (b, i, j)),
            pl.BlockSpec((1, KV_WIDTH, t), lambda b, j, i: (b, 0, 0)),
            pl.BlockSpec((1, t, KV_WIDTH), lambda b, j, i: (b, 0, 0)),
        ],
        out_specs=pl.BlockSpec((1, tq, width), lambda b, j, i: (b, i, j)),
        out_shape=jax.ShapeDtypeStruct(q.shape, BF16),
        scratch_shapes=[pltpu.VMEM((N_KV_HEADS, t, MXU_N), BF16)],
        compiler_params=_params(3),
        name=f"attn_t{t}",
    )(q, kt, v)


def _attn_pipe_kernel(steps_per_batch, q_ref, kt_ref, v_ref, o_ref, vm_ref, pa_ref, pb_ref, s_ref):
    t = pl.program_id(0)

    @pl.when(t == 0)
    def _():
        pb_ref[...] = jnp.ones_like(pb_ref)

    @pl.when((t == 0) | ((t - 1) % steps_per_batch == 0))
    def _():
        _build_masked_v(v_ref, vm_ref)

    n_chunks = kt_ref.shape[-1] // KEY_CHUNK
    slot_of_lane = jax.lax.broadcasted_iota(jnp.int32, (1, LANES), 1) // HEAD_DIM

    def keys(c):
        return slice(c * KEY_CHUNK, (c + 1) * KEY_CHUNK)

    def score_chunk(g, c, run_max):
        lo = (g // 2) * LANES
        q_pair = q_ref[0, :, lo:lo + LANES]
        q_one = jnp.where(slot_of_lane == g % 2, q_pair, jnp.zeros_like(q_pair))
        s = _dot(q_one, kt_ref[0, lo:lo + LANES, keys(c)])
        s_ref[g % 2, :, keys(c)] = s
        for k in range(KEY_CHUNK // LANES):
            part = s[:, k * LANES:(k + 1) * LANES]
            run_max = part if run_max is None else jnp.maximum(run_max, part)
        return run_max

    def step(p_new, p_old):
        run_max = None
        for c in range(n_chunks):
            run_max = score_chunk(0, c, run_max)
        out = None
        den = None
        for g in range(N_KV_HEADS):
            row_max = jnp.max(run_max, axis=-1, keepdims=True)
            run_max = None
            acc = None
            for c in range(n_chunks):
                p_new[g, :, keys(c)] = jnp.exp2(s_ref[g % 2, :, keys(c)] - row_max).astype(BF16)
                if g + 1 < N_KV_HEADS:
                    run_max = score_chunk(g + 1, c, run_max)
                pv = _dot(p_old[g, :, keys(c)], vm_ref[g, keys(c), :])
                acc = pv if acc is None else acc + pv
            own = _group_of_lane() == g
            row_sum = acc[:, _sum_lane(g):_sum_lane(g) + 1]
            out = acc if out is None else jnp.where(own, acc, out)
            den = row_sum if den is None else jnp.where(own, row_sum, den)
        o_ref[0] = (out / den).astype(BF16)

    @pl.when(t % 2 == 0)
    def _():
        step(pa_ref, pb_ref)

    @pl.when(t % 2 == 1)
    def _():
        step(pb_ref, pa_ref)


def _attention_pipelined(q, kt, v, tq):
    bsz, n, _ = q.shape
    t_keys = kt.shape[-1]
    tiles = n // tq
    steps_per_batch = GROUP * tiles
    n_items = bsz * steps_per_batch

    def item(t):
        return t // steps_per_batch, (t // tiles) % GROUP, t % tiles

    def cur(t):
        return item(jnp.minimum(t, n_items - 1))

    def prev(t):
        return item(jnp.maximum(t - 1, 0))

    def q_map(t):
        b, j, i = cur(t)
        return b, i, j

    def o_map(t):
        b, j, i = prev(t)
        return b, i, j

    return pl.pallas_call(
        functools.partial(_attn_pipe_kernel, steps_per_batch),
        grid=(n_items + 1,),
        in_specs=[
            pl.BlockSpec((1, tq, MXU_N), q_map),
            pl.BlockSpec((1, KV_WIDTH, t_keys), lambda t: (cur(t)[0], 0, 0)),
            pl.BlockSpec((1, t_keys, KV_WIDTH), lambda t: (prev(t)[0], 0, 0)),
        ],
        out_specs=pl.BlockSpec((1, tq, MXU_N), o_map),
        out_shape=jax.ShapeDtypeStruct(q.shape, BF16),
        scratch_shapes=[pltpu.VMEM((N_KV_HEADS, t_keys, MXU_N), BF16),
                        pltpu.VMEM((N_KV_HEADS, tq, t_keys), BF16), pltpu.VMEM((N_KV_HEADS, tq, t_keys), BF16),
                        pltpu.VMEM((2, tq, t_keys), F32)],
        compiler_params=_params(1),
        name=f"attn_pipe_t{t_keys}",
    )(q, kt, v)


def _shifted_rows(x, prev_row, next_row, period):
    tm = x.shape[0]
    t = jax.lax.broadcasted_iota(jnp.int32, (tm, 1), 0) % period
    before = jnp.where(t == 0, prev_row, pltpu.roll(x, 1, axis=0))
    after = jnp.where(t == period - 1, next_row, pltpu.roll(x, tm - 1, axis=0))
    return before, after


def _mix_out_kernel(tiles_per_seq, period, att_ref, b_ref, y_ref, yp_ref, yn_ref, sa_ref, sg_ref, h_ref, g1_ref, sh2_ref,
                    sc2_ref, gpost_ref, gpre_ref, cw_ref, wa_ref, wc_ref, wo_ref, h1_ref, u2_ref):
    i = pl.program_id(0)
    tm = y_ref.shape[0]
    halves = [slice(0, tm // 2), slice(tm // 2, tm)]
    att = [_dot(att_ref[r, :], wa_ref[...]) for r in halves]

    has_prev = (i % tiles_per_seq != 0).astype(F32)
    has_next = (i % tiles_per_seq != tiles_per_seq - 1).astype(F32)
    y = y_ref[...].astype(F32)
    y_prev = yp_ref[HALO - 1:HALO, :].astype(F32) * has_prev
    y_next = yn_ref[0:1, :].astype(F32) * has_next
    before, after = _shifted_rows(y, y_prev, y_next, period)
    conv = cw_ref[0:1, :] * before + cw_ref[1:2, :] * y + cw_ref[2:3, :] * after
    conv_in = (b_ref[...].astype(F32) * conv).astype(BF16)
    cnv = [_dot(conv_in[r, :], wc_ref[...]) for r in halves]

    mixed = []
    half = D_MODEL // 2
    for r, a, c in zip(halves, att, cnv):
        merged = (sa_ref[r, :].astype(F32) * a + sg_ref[r, :].astype(F32) * c).astype(BF16)
        mixed.append(jnp.concatenate([_dot(merged, wo_ref[:, n * half:(n + 1) * half]) for n in range(2)], axis=1))

    for r, mo in zip(halves, mixed):
        h1 = h_ref[r, :] + g1_ref[0] * (_rms_rows(mo) * gpost_ref[...])
        h1_ref[r, :] = h1
        u2 = _rms_rows(h1) * (gpre_ref[...] * (1.0 + sc2_ref[0])) + sh2_ref[0]
        u2_ref[r, :] = u2.astype(BF16)


def _halo_specs(tm, n_rows, width):
    per = tm // HALO
    last = n_rows // HALO - 1
    prev = pl.BlockSpec((HALO, width), lambda i: (jnp.maximum(i * per - 1, 0), 0))
    nxt = pl.BlockSpec((HALO, width), lambda i: (jnp.minimum((i + 1) * per, last), 0))
    return prev, nxt


def _mix_out(att, b, y, sa, sg, h, ada3, ada_row, seq_len, tm, g_post1, g_pre2, conv_w, w_att_out, w_conv_out, w_o):
    n = h.shape[0]
    assert seq_len % tm == 0 or tm % seq_len == 0
    tiles_per_seq = max(seq_len // tm, 1)
    row = lambda i: (i, 0)
    wide = pl.BlockSpec((tm, D_MODEL), row)
    prev, nxt = _halo_specs(tm, n, D_MODEL)
    ada = lambda k: pl.BlockSpec((1, 1, D_MODEL), lambda i: (ada_row(i), 0, k))
    sq = (D_MODEL, D_MODEL)
    return pl.pallas_call(
        functools.partial(_mix_out_kernel, tiles_per_seq, min(seq_len, tm)),
        grid=(n // tm,),
        in_specs=[wide, wide, wide, prev, nxt, wide, wide, wide, ada(2), ada(3), ada(4),
                  _resident((1, D_MODEL)), _resident((1, D_MODEL)), _resident((3, D_MODEL)),
                  _resident(sq), _resident(sq), _resident(sq)],
        out_specs=[wide, wide],
        out_shape=[jax.ShapeDtypeStruct((n, D_MODEL), F32), jax.ShapeDtypeStruct((n, D_MODEL), BF16)],
        compiler_params=_params(1),
        name=f"mix_out_s{seq_len}",
    )(att, b, y, y, y, sa, sg, h, ada3, ada3, ada3, g_post1, g_pre2, conv_w, w_att_out, w_conv_out, w_o)


FF_CHUNKS = (4 * MXU_N, 4 * MXU_N, 3 * MXU_N)
assert sum(FF_CHUNKS) == D_FF


def _ffn_kernel(tiles_per_seq, u_ref, up_ref, un_ref, h_ref, g2_ref, gpost_ref, cw_ref, wup_ref, wdn_ref, o_ref, ext_ref):
    i = pl.program_id(0)
    tm = u_ref.shape[0]
    has_prev = i % tiles_per_seq != 0
    has_next = i % tiles_per_seq != tiles_per_seq - 1
    zeros = jnp.zeros((HALO, D_MODEL), BF16)
    ext_ref[0:HALO, :] = jnp.where(has_prev, up_ref[...], zeros)
    ext_ref[HALO:HALO + tm, :] = u_ref[...]
    ext_ref[HALO + tm:, :] = jnp.where(has_next, un_ref[...], zeros)
    ext = ext_ref[...]
    rows = tm + 2 * HALO

    def up(lo, width):
        return [_dot(ext, wup_ref[:, col:col + width]) for col in (lo, D_FF + lo)]

    def conv(z, col, width):
        w = cw_ref[:, col:col + width]
        before = pltpu.roll(z, 1, axis=0)[HALO:HALO + tm]
        after = pltpu.roll(z, rows - 1, axis=0)[HALO:HALO + tm]
        return w[0:1] * before + w[1:2] * z[HALO:HALO + tm] + w[2:3] * after

    def down(lo, width, z_gate, z_val):
        gate = conv(z_gate, lo, width)
        val = conv(z_val, D_FF + lo, width)
        act = (gate * jax.nn.sigmoid(gate) * val).astype(BF16)
        half = D_MODEL // 2
        return [_dot(act, wdn_ref[lo:lo + width, n * half:(n + 1) * half]) for n in range(2)]

    starts = [sum(FF_CHUNKS[:c]) for c in range(len(FF_CHUNKS))]
    ff = None
    z_next = up(starts[0], FF_CHUNKS[0])
    for c, (lo, width) in enumerate(zip(starts, FF_CHUNKS)):
        z_cur = z_next
        if c + 1 < len(FF_CHUNKS):
            z_next = up(starts[c + 1], FF_CHUNKS[c + 1])
        parts = down(lo, width, *z_cur)
        ff = parts if ff is None else [a + b for a, b in zip(ff, parts)]
    ff = jnp.concatenate(ff, axis=1)

    o_ref[...] = h_ref[...] + g2_ref[0] * (_rms_rows(ff) * gpost_ref[...])


def _ffn(u2, h1, ada3, ada_row, seq_len, tm, g_post2, conv_ffn, w_up, w_down):
    n = h1.shape[0]
    assert seq_len % tm == 0
    row = lambda i: (i, 0)
    wide = pl.BlockSpec((tm, D_MODEL), row)
    prev, nxt = _halo_specs(tm, n, D_MODEL)
    return pl.pallas_call(
        functools.partial(_ffn_kernel, seq_len // tm),
        grid=(n // tm,),
        in_specs=[wide, prev, nxt, wide, pl.BlockSpec((1, 1, D_MODEL), lambda i: (ada_row(i), 0, 5)),
                  _resident((1, D_MODEL)), _resident((3, 2 * D_FF)),
                  _resident((D_MODEL, 2 * D_FF)), _resident((D_FF, D_MODEL))],
        out_specs=wide,
        out_shape=jax.ShapeDtypeStruct((n, D_MODEL), F32),
        scratch_shapes=[pltpu.VMEM((tm + 2 * HALO, D_MODEL), BF16)],
        compiler_params=_params(1),
        name=f"ffn_s{seq_len}",
    )(u2, u2, u2, h1, ada3, g_post2, conv_ffn, w_up, w_down)


def _rope_tables(n):
    pos = np.arange(n)
    inv = np.power(ROPE_THETA, -np.arange(0, AXIS_DIM, 2, dtype=np.float64) / AXIS_DIM)
    ang_r = (pos // GRID_W)[:, None] * inv[None, :]
    ang_c = (pos % GRID_W)[:, None] * inv[None, :]
    ang = np.concatenate([ang_r, ang_r, ang_c, ang_c], axis=1)
    sign = np.tile(np.concatenate([-np.ones(AXIS_DIM // 2), np.ones(AXIS_DIM // 2)]), 2)
    reps = LANES // HEAD_DIM
    cos = np.tile(np.cos(ang), (1, reps)).astype(np.float32)
    sin = np.tile(np.sin(ang) * sign[None, :], (1, reps)).astype(np.float32)
    return jnp.asarray(cos), jnp.asarray(sin)


def _segment_mean_matrix():
    head = np.arange(MXU_N) // HEAD_DIM
    return jnp.asarray((head[:, None] == head[None, :]).astype(np.float32) / HEAD_DIM, dtype=BF16)


def _layer(x, ada3, ada_row, seq_len, tiles, w, rope_tabs, cache_kv):
    bsz = x.shape[0]
    n = bsz * seq_len
    xf = x.reshape(n, D_MODEL)
    tm_in, tq, gs, tm_mix, tm_ffn = tiles
    kv_dtype = BF16 if cache_kv is not None else F32
    q, k, v, b, y, sa, sg = _in_proj(xf, ada3, lambda i: ada_row(i, tm_in), seq_len, tm_in, w["w_in"], w["g_pre1"],
                                     w["qg"], w["kg"], w["seg"], rope_tabs, kv_dtype)
    k3 = k.reshape(bsz, seq_len, KV_WIDTH).astype(BF16)
    v3 = v.reshape(bsz, seq_len, KV_WIDTH).astype(BF16)
    if cache_kv is not None:
        ck, cv = cache_kv
        k3 = jnp.concatenate([ck.reshape(bsz, -1, KV_WIDTH).astype(BF16), k3], axis=1)
        v3 = jnp.concatenate([cv.reshape(bsz, -1, KV_WIDTH).astype(BF16), v3], axis=1)
    t = k3.shape[1]
    kt = k3.transpose(0, 2, 1)
    q3 = q.reshape(bsz, seq_len, ATT_WIDTH)
    att = _attention_pipelined(q3, kt, v3, tq) if gs is None else _attention(q3, kt, v3, tq, gs)
    att = att.reshape(n, ATT_WIDTH)
    h1, u2 = _mix_out(att, b, y, sa, sg, xf, ada3, lambda i: ada_row(i, tm_mix), seq_len, tm_mix, w["g_post1"],
                      w["g_pre2"], w["conv_w"], w["w_att_out"], w["w_conv_out"], w["w_o"])
    out = _ffn(u2, h1, ada3, lambda i: ada_row(i, tm_ffn), seq_len, tm_ffn, w["g_post2"], w["conv_ffn"], w["w_up"],
               w["w_down"])
    return out.reshape(x.shape), k, v


def kernel(x_prompt, x_sample, cache_k, cache_v, c, c_ctx, w_ada, b_ada, g_pre1, g_post1, g_pre2, g_post2, w_in, q_norm,
           k_norm, w_att_out, conv_w, w_conv_out, w_o, w_up, conv_ffn, w_down):
    depth = w_in.shape[0]
    dec_batch, dec_seq, _ = x_sample.shape
    batch, seq, _ = x_prompt.shape
    rope_tabs = _rope_tables(dec_seq)
    seg = _segment_mean_matrix()
    cc = jnp.zeros((ADA_ROWS, D_MODEL), F32).at[0].set(c_ctx).at[1:1 + dec_batch].set(c)
    reps = MXU_N // HEAD_DIM

    h_p, h_s = x_prompt, x_sample
    new_ks, new_vs = [], []
    for i in range(depth):
        wq = w_in[i][:, :ATT_WIDTH].reshape(D_MODEL, N_KV_HEADS, GROUP, HEAD_DIM).transpose(0, 2, 1, 3)
        wa = w_att_out[i].reshape(N_KV_HEADS, GROUP, HEAD_DIM, D_MODEL).transpose(1, 0, 2, 3)
        w = {
            "g_pre1": g_pre1[i][None], "g_post1": g_post1[i][None], "g_pre2": g_pre2[i][None], "g_post2": g_post2[i][None],
            "w_in": jnp.concatenate([wq.reshape(D_MODEL, ATT_WIDTH), w_in[i][:, ATT_WIDTH:]], axis=1).astype(BF16),
            "qg": jnp.tile(q_norm[i] * (HEAD_DIM ** -0.5 * LOG2E), reps)[None],
            "kg": jnp.tile(k_norm[i], reps)[None],
            "seg": seg,
            "w_att_out": wa.reshape(ATT_WIDTH, D_MODEL).astype(BF16), "conv_w": conv_w[i],
            "w_conv_out": w_conv_out[i].astype(BF16),
            "w_o": w_o[i].astype(BF16), "w_up": w_up[i].astype(BF16), "conv_ffn": conv_ffn[i],
            "w_down": w_down[i].astype(BF16),
        }
        ada3 = _ada(cc, w_ada[i], b_ada[i]).reshape(ADA_ROWS, N_ADA, D_MODEL).reshape(ADA_ROWS, 1, N_ADA * D_MODEL)
        h_p, k_ctx, v_ctx = _layer(h_p, ada3, lambda t, tm: 0, seq, (512, seq, GROUP, 512, seq), w, None, None)
        new_ks.append(k_ctx.reshape(batch, seq, N_KV_HEADS, HEAD_DIM))
        new_vs.append(v_ctx.reshape(batch, seq, N_KV_HEADS, HEAD_DIM))
        h_s, _, _ = _layer(h_s, ada3, lambda t, tm: 1 + t // (dec_seq // tm), dec_seq, (512, 256, None, 512, 512), w,
                           rope_tabs, (cache_k[:, i], cache_v[:, i]))
    return (h_p, h_s, jnp.stack(new_ks, axis=1), jnp.stack(new_vs, axis=1))
```

```python
import functools

import numpy as np
import jax
import jax.numpy as jnp
from jax.experimental import pallas as pl
from jax.experimental.pallas import tpu as pltpu

D_MODEL = 1024
N_HEADS = 16
N_KV_HEADS = 4
HEAD_DIM = 64
GROUP = N_HEADS // N_KV_HEADS
ATT_WIDTH = N_HEADS * HEAD_DIM
KV_WIDTH = N_KV_HEADS * HEAD_DIM
D_FF = 2816
GRID_W = 64
ROPE_THETA = 10000.0
AXIS_DIM = HEAD_DIM // 2
N_ADA = 6
EPS = 1e-6
LOG2E = 1.4426950408889634

OFF_Q = 0
OFF_K = OFF_Q + ATT_WIDTH
OFF_V = OFF_K + KV_WIDTH
OFF_B = OFF_V + KV_WIDTH
OFF_C = OFF_B + D_MODEL
OFF_X = OFF_C + D_MODEL
OFF_GA = OFF_X + D_MODEL
OFF_GC = OFF_GA + D_MODEL
IN_WIDTH = OFF_GC + D_MODEL

LANES = 128
MXU_N = 256
HALO = 16
ADA_ROWS = 16
VMEM_LIMIT = 56 * 1024 * 1024
KEY_CHUNK = 512

BF16 = jnp.bfloat16
F32 = jnp.float32


def _dot(a, b):
    return jnp.dot(a, b, preferred_element_type=F32)


def _resident(shape):
    nd = len(shape)
    return pl.BlockSpec(shape, lambda *_: (0,) * nd, pipeline_mode=pl.Buffered(1))


def _params(n_axes):
    return pltpu.CompilerParams(dimension_semantics=("arbitrary",) * n_axes, vmem_limit_bytes=VMEM_LIMIT)


def _ada_kernel(c_ref, w_ref, b_ref, o_ref):
    c = c_ref[...]
    s = (c * jax.nn.sigmoid(c)).astype(BF16)
    o_ref[...] = _dot(s, w_ref[...].astype(BF16)) + b_ref[...]


def _ada(cc, w_ada, b_ada):
    n = w_ada.shape[1]
    tn = D_MODEL
    return pl.pallas_call(
        _ada_kernel,
        grid=(n // tn,),
        in_specs=[
            pl.BlockSpec((ADA_ROWS, D_MODEL), lambda j: (0, 0)),
            pl.BlockSpec((D_MODEL, tn), lambda j: (0, j)),
            pl.BlockSpec((1, tn), lambda j: (0, j)),
        ],
        out_specs=pl.BlockSpec((ADA_ROWS, tn), lambda j: (0, j)),
        out_shape=jax.ShapeDtypeStruct((ADA_ROWS, n), F32),
        compiler_params=_params(1),
        name="ada",
    )(cc, w_ada, b_ada.reshape(1, n))


def _rms_rows(x):
    return x * jax.lax.rsqrt(jnp.mean(x * x, axis=-1, keepdims=True) + EPS)


def _head_norm(z, seg_ref, gain):
    ms = _dot((z * z).astype(BF16), seg_ref[...])
    return z * jax.lax.rsqrt(ms + EPS) * gain


def _rope(z, cos, sin, first_half):
    partner = jnp.where(first_half, pltpu.roll(z, LANES - AXIS_DIM // 2, axis=1), pltpu.roll(z, AXIS_DIM // 2, axis=1))
    return z * cos + partner * sin


def _in_proj_kernel(rope, x_ref, sh_ref, sc_ref, gpre_ref, wq_ref, w_ref, qg_ref, kg_ref, seg_ref, *rest):
    if rope:
        cos_ref, sin_ref, q_ref, kt_ref, v_ref, b_ref, y_ref, sa_ref, sg_ref = rest
        cos = cos_ref[...]
        sin = sin_ref[...]
        lane = jax.lax.broadcasted_iota(jnp.int32, (1, LANES), 1)
        first_half = (lane % AXIS_DIM) < (AXIS_DIM // 2)
    else:
        q_ref, kt_ref, v_ref, b_ref, y_ref, sa_ref, sg_ref, ktf_ref, vtf_ref = rest

    x = x_ref[...]
    mod = gpre_ref[...] * (1.0 + sc_ref[0])
    u = (_rms_rows(x) * mod + sh_ref[0]).astype(BF16)

    def rotate(z):
        if not rope:
            return z
        return jnp.concatenate(
            [_rope(z[:, h * LANES:(h + 1) * LANES], cos, sin, first_half) for h in range(MXU_N // LANES)], axis=1)

    def cols(j):
        return slice(j * MXU_N, (j + 1) * MXU_N)

    def store_per_sequence(ref, zt):
        seq = ref.shape[-1]
        for s in range(ref.shape[0]):
            ref[s] = zt[:, s * seq:(s + 1) * seq]

    def store_q(j, z):
        q_ref[:, cols(j)] = rotate(_head_norm(z, seg_ref, qg_ref[...])).astype(BF16)

    def store_k(z):
        k = _head_norm(z, seg_ref, kg_ref[...])
        kt_ref[...] = rotate(k).T.astype(BF16)
        if not rope:
            store_per_sequence(ktf_ref, k.T)

    def store_v(z):
        v_ref[...] = z.astype(BF16)
        if not rope:
            store_per_sequence(vtf_ref, z.T)

    def store_b(j, z):
        b_ref[:, cols(j)] = z.astype(BF16)

    def store_y(j, zc, zx):
        y_ref[:, cols(j)] = (zc * zx).astype(BF16)

    def store_gate(ref, j, z):
        ref[:, cols(j)] = jax.nn.sigmoid(z).astype(BF16)

    work = [(wq_ref, (j * MXU_N,), functools.partial(store_q, j)) for j in range(ATT_WIDTH // MXU_N)]
    work += [(w_ref, (OFF_K,), store_k), (w_ref, (OFF_V,), store_v)]
    for j in range(D_MODEL // MXU_N):
        work += [(w_ref, (OFF_B + j * MXU_N,), functools.partial(store_b, j)),
                 (w_ref, (OFF_C + j * MXU_N, OFF_X + j * MXU_N), functools.partial(store_y, j)),
                 (w_ref, (OFF_GA + j * MXU_N,), functools.partial(store_gate, sa_ref, j)),
                 (w_ref, (OFF_GC + j * MXU_N,), functools.partial(store_gate, sg_ref, j))]
    pending = None
    for weights, offsets, consume in work:
        products = [_dot(u, weights[:, lo:lo + MXU_N]) for lo in offsets]
        if pending is not None:
            pending[0](*pending[1])
        pending = (consume, products)
    pending[0](*pending[1])


def _in_proj(x, ada3, ada_row, seq_len, tm, wq, w_in, g_pre1, qg, kg, seg, rope_tabs):
    n = x.shape[0]
    rope = rope_tabs is not None
    row = lambda i: (i, 0)
    in_specs = [
        pl.BlockSpec((tm, D_MODEL), row),
        pl.BlockSpec((1, 1, D_MODEL), lambda i: (ada_row(i), 0, 0)),
        pl.BlockSpec((1, 1, D_MODEL), lambda i: (ada_row(i), 0, 1)),
        _resident((1, D_MODEL)),
        _resident((D_MODEL, ATT_WIDTH)),
        _resident((D_MODEL, IN_WIDTH)),
        _resident((1, MXU_N)),
        _resident((1, MXU_N)),
        _resident((MXU_N, MXU_N)),
    ]
    args = [x, ada3, ada3, g_pre1, wq, w_in, qg, kg, seg]
    wide = jax.ShapeDtypeStruct((n, D_MODEL), BF16)
    wide_spec = pl.BlockSpec((tm, D_MODEL), row)
    out_shape = [wide, jax.ShapeDtypeStruct((KV_WIDTH, n), BF16), jax.ShapeDtypeStruct((n, KV_WIDTH), BF16)] + [wide] * 4
    out_specs = [wide_spec, pl.BlockSpec((KV_WIDTH, tm), lambda i: (0, i)), pl.BlockSpec((tm, KV_WIDTH), row)]
    out_specs += [wide_spec] * 4
    if rope:
        assert seq_len % tm == 0
        tiles_per_seq = seq_len // tm
        in_specs += [pl.BlockSpec((tm, LANES), lambda i: (i % tiles_per_seq, 0))] * 2
        args += list(rope_tabs)
    else:
        assert tm % seq_len == 0
        per_tile = tm // seq_len
        transposed = jax.ShapeDtypeStruct((n // seq_len, KV_WIDTH, seq_len), F32)
        out_shape += [transposed] * 2
        out_specs += [pl.BlockSpec((per_tile, KV_WIDTH, seq_len), lambda i: (i, 0, 0))] * 2
    return pl.pallas_call(
        functools.partial(_in_proj_kernel, rope),
        grid=(n // tm,),
        in_specs=in_specs,
        out_specs=out_specs,
        out_shape=out_shape,
        compiler_params=_params(1),
        name="in_proj_rope" if rope else "in_proj",
    )(*args)


def _scores(q_ref, kt_ref, col0, g):
    lo = (g // 2) * LANES
    q_pair = q_ref[0, :, col0 + lo:col0 + lo + LANES]
    slot = jax.lax.broadcasted_iota(jnp.int32, (1, LANES), 1) // HEAD_DIM
    q_one = jnp.where(slot == g % 2, q_pair, jnp.zeros_like(q_pair))
    return _dot(q_one, kt_ref[lo:lo + LANES, :])


def _group_of_lane():
    return jax.lax.broadcasted_iota(jnp.int32, (1, MXU_N), 1) // HEAD_DIM


def _sum_lane(g):
    return (HEAD_DIM * (g + 1)) % MXU_N


def _build_masked_v(pieces, vm_ref):
    lane = jax.lax.broadcasted_iota(jnp.int32, (1, MXU_N), 1)
    for row0, v in pieces:
        v = v.astype(F32)
        for g in range(N_KV_HEADS):
            ones_col = (lane == _sum_lane(g)).astype(F32)
            vm_ref[g, row0:row0 + v.shape[0], :] = jnp.where(_group_of_lane() == g, v, ones_col).astype(BF16)


def _unnormalised_probs(s):
    return jnp.exp2(s - jnp.max(s, axis=-1, keepdims=True)).astype(BF16)


def _weighted_values(probs, vm_ref):
    acc = None
    den = None
    for g in range(N_KV_HEADS):
        pv = _dot(probs[g], vm_ref[g])
        own = _group_of_lane() == g
        row_sum = pv[:, _sum_lane(g):_sum_lane(g) + 1]
        acc = pv if acc is None else jnp.where(own, pv, acc)
        den = row_sum if den is None else jnp.where(own, row_sum, den)
    return acc / den


def _attn_kernel(n_sets, q_ref, kt_ref, v_ref, o_ref, vm_ref):
    @pl.when((pl.program_id(1) == 0) & (pl.program_id(2) == 0))
    def _():
        _build_masked_v([(0, v_ref[...])], vm_ref)

    for j in range(n_sets):
        probs = [_unnormalised_probs(_scores(q_ref, kt_ref, j * MXU_N, g)) for g in range(N_KV_HEADS)]
        o_ref[0, :, j * MXU_N:(j + 1) * MXU_N] = _weighted_values(probs, vm_ref).astype(BF16)


def _attention(q, kt, v, tq, sets_per_step):
    bsz, n, _ = q.shape
    t = kt.shape[-1] // bsz
    width = sets_per_step * MXU_N
    return pl.pallas_call(
        functools.partial(_attn_kernel, sets_per_step),
        grid=(bsz, GROUP // sets_per_step, n // tq),
        in_specs=[
            pl.BlockSpec((1, tq, width), lambda b, j, i: (b, i, j)),
            pl.BlockSpec((KV_WIDTH, t), lambda b, j, i: (0, b)),
            pl.BlockSpec((t, KV_WIDTH), lambda b, j, i: (b, 0)),
        ],
        out_specs=pl.BlockSpec((1, tq, width), lambda b, j, i: (b, i, j)),
        out_shape=jax.ShapeDtypeStruct(q.shape, BF16),
        scratch_shapes=[pltpu.VMEM((N_KV_HEADS, t, MXU_N), BF16)],
        compiler_params=_params(3),
        name=f"attn_t{t}",
    )(q, kt, v)


def _attn_pipe_kernel(steps_per_batch, q_ref, ktc_ref, ktn_ref, vc_ref, vn_ref, o_ref, vm_ref, pa_ref, pb_ref, s_ref):
    t = pl.program_id(0)

    @pl.when(t == 0)
    def _():
        pb_ref[...] = jnp.ones_like(pb_ref)

    @pl.when((t == 0) | ((t - 1) % steps_per_batch == 0))
    def _():
        _build_masked_v([(0, vc_ref[0]), (vc_ref.shape[1], vn_ref[...])], vm_ref)

    cached_chunks = ktc_ref.shape[-1] // KEY_CHUNK
    n_chunks = cached_chunks + ktn_ref.shape[-1] // KEY_CHUNK
    slot_of_lane = jax.lax.broadcasted_iota(jnp.int32, (1, LANES), 1) // HEAD_DIM

    def keys(c):
        return slice(c * KEY_CHUNK, (c + 1) * KEY_CHUNK)

    def score_chunk(g, c, run_max):
        lo = (g // 2) * LANES
        q_pair = q_ref[0, :, lo:lo + LANES]
        q_one = jnp.where(slot_of_lane == g % 2, q_pair, jnp.zeros_like(q_pair))
        if c < cached_chunks:
            kt = ktc_ref[0, lo:lo + LANES, keys(c)]
        else:
            kt = ktn_ref[lo:lo + LANES, keys(c - cached_chunks)]
        s = _dot(q_one, kt)
        s_ref[g % 2, :, keys(c)] = s
        for k in range(KEY_CHUNK // LANES):
            part = s[:, k * LANES:(k + 1) * LANES]
            run_max = part if run_max is None else jnp.maximum(run_max, part)
        return run_max

    def step(p_new, p_old):
        run_max = None
        for c in range(n_chunks):
            run_max = score_chunk(0, c, run_max)
        out = None
        den = None
        for g in range(N_KV_HEADS):
            row_max = jnp.max(run_max, axis=-1, keepdims=True)
            run_max = None
            acc = None
            for c in range(n_chunks):
                p_new[g, :, keys(c)] = jnp.exp2(s_ref[g % 2, :, keys(c)] - row_max).astype(BF16)
                if g + 1 < N_KV_HEADS:
                    run_max = score_chunk(g + 1, c, run_max)
                pv = _dot(p_old[g, :, keys(c)], vm_ref[g, keys(c), :])
                acc = pv if acc is None else acc + pv
            own = _group_of_lane() == g
            row_sum = acc[:, _sum_lane(g):_sum_lane(g) + 1]
            out = acc if out is None else jnp.where(own, acc, out)
            den = row_sum if den is None else jnp.where(own, row_sum, den)
        o_ref[0] = (out / den).astype(BF16)

    @pl.when(t % 2 == 0)
    def _():
        step(pa_ref, pb_ref)

    @pl.when(t % 2 == 1)
    def _():
        step(pb_ref, pa_ref)


def _attention_pipelined(q, ktc, ktn, vc, vn, tq):
    bsz, n, _ = q.shape
    t_cached = ktc.shape[-1]
    assert t_cached % KEY_CHUNK == 0 and n % KEY_CHUNK == 0
    t_keys = t_cached + n
    tiles = n // tq
    steps_per_batch = GROUP * tiles
    n_items = bsz * steps_per_batch

    def item(t):
        return t // steps_per_batch, (t // tiles) % GROUP, t % tiles

    def cur(t):
        return item(jnp.minimum(t, n_items - 1))

    def prev(t):
        return item(jnp.clip(t - 1, 0, n_items - 1))

    def q_map(t):
        b, j, i = cur(t)
        return b, i, j

    def o_map(t):
        b, j, i = prev(t)
        return b, i, j

    return pl.pallas_call(
        functools.partial(_attn_pipe_kernel, steps_per_batch),
        grid=(n_items + 1,),
        in_specs=[
            pl.BlockSpec((1, tq, MXU_N), q_map),
            pl.BlockSpec((1, KV_WIDTH, t_cached), lambda t: (cur(t)[0], 0, 0)),
            pl.BlockSpec((KV_WIDTH, n), lambda t: (0, cur(t)[0])),
            pl.BlockSpec((1, t_cached, KV_WIDTH), lambda t: (prev(t)[0], 0, 0)),
            pl.BlockSpec((n, KV_WIDTH), lambda t: (prev(t)[0], 0)),
        ],
        out_specs=pl.BlockSpec((1, tq, MXU_N), o_map),
        out_shape=jax.ShapeDtypeStruct(q.shape, BF16),
        scratch_shapes=[pltpu.VMEM((N_KV_HEADS, t_keys, MXU_N), BF16),
                        pltpu.VMEM((N_KV_HEADS, tq, t_keys), BF16), pltpu.VMEM((N_KV_HEADS, tq, t_keys), BF16),
                        pltpu.VMEM((2, tq, t_keys), F32)],
        compiler_params=_params(1),
        name=f"attn_pipe_t{t_keys}",
    )(q, ktc, ktn, vc, vn)


def _shifted_rows(x, prev_row, next_row, period):
    tm = x.shape[0]
    t = jax.lax.broadcasted_iota(jnp.int32, (tm, 1), 0) % period
    before = jnp.where(t == 0, prev_row, pltpu.roll(x, 1, axis=0))
    after = jnp.where(t == period - 1, next_row, pltpu.roll(x, tm - 1, axis=0))
    return before, after


def _mix_out_kernel(tiles_per_seq, period, att_ref, b_ref, y_ref, yp_ref, yn_ref, sa_ref, sg_ref, h_ref, g1_ref, sh2_ref,
                    sc2_ref, gpost_ref, gpre_ref, cw_ref, wa_ref, wc_ref, wo_ref, h1_ref, u2_ref):
    i = pl.program_id(0)
    tm = y_ref.shape[0]
    halves = [slice(0, tm // 2), slice(tm // 2, tm)]
    att = [_dot(att_ref[r, :], wa_ref[...]) for r in halves]

    has_prev = (i % tiles_per_seq != 0).astype(F32)
    has_next = (i % tiles_per_seq != tiles_per_seq - 1).astype(F32)
    y = y_ref[...].astype(F32)
    y_prev = yp_ref[HALO - 1:HALO, :].astype(F32) * has_prev
    y_next = yn_ref[0:1, :].astype(F32) * has_next
    before, after = _shifted_rows(y, y_prev, y_next, period)
    conv = cw_ref[0:1, :] * before + cw_ref[1:2, :] * y + cw_ref[2:3, :] * after
    conv_in = (b_ref[...].astype(F32) * conv).astype(BF16)
    cnv = [_dot(conv_in[r, :], wc_ref[...]) for r in halves]

    mixed = []
    half = D_MODEL // 2
    for r, a, c in zip(halves, att, cnv):
        merged = (sa_ref[r, :].astype(F32) * a + sg_ref[r, :].astype(F32) * c).astype(BF16)
        mixed.append(jnp.concatenate([_dot(merged, wo_ref[:, n * half:(n + 1) * half]) for n in range(2)], axis=1))

    for r, mo in zip(halves, mixed):
        h1 = h_ref[r, :] + g1_ref[0] * (_rms_rows(mo) * gpost_ref[...])
        h1_ref[r, :] = h1
        u2 = _rms_rows(h1) * (gpre_ref[...] * (1.0 + sc2_ref[0])) + sh2_ref[0]
        u2_ref[r, :] = u2.astype(BF16)


def _halo_specs(tm, n_rows, width):
    per = tm // HALO
    last = n_rows // HALO - 1
    prev = pl.BlockSpec((HALO, width), lambda i: (jnp.maximum(i * per - 1, 0), 0))
    nxt = pl.BlockSpec((HALO, width), lambda i: (jnp.minimum((i + 1) * per, last), 0))
    return prev, nxt


def _mix_out(att, b, y, sa, sg, h, ada3, ada_row, seq_len, tm, g_post1, g_pre2, conv_w, w_att_out, w_conv_out, w_o):
    n = h.shape[0]
    assert seq_len % tm == 0 or tm % seq_len == 0
    tiles_per_seq = max(seq_len // tm, 1)
    row = lambda i: (i, 0)
    wide = pl.BlockSpec((tm, D_MODEL), row)
    prev, nxt = _halo_specs(tm, n, D_MODEL)
    ada = lambda k: pl.BlockSpec((1, 1, D_MODEL), lambda i: (ada_row(i), 0, k))
    sq = (D_MODEL, D_MODEL)
    return pl.pallas_call(
        functools.partial(_mix_out_kernel, tiles_per_seq, min(seq_len, tm)),
        grid=(n // tm,),
        in_specs=[wide, wide, wide, prev, nxt, wide, wide, wide, ada(2), ada(3), ada(4),
                  _resident((1, D_MODEL)), _resident((1, D_MODEL)), _resident((3, D_MODEL)),
                  _resident(sq), _resident(sq), _resident(sq)],
        out_specs=[wide, wide],
        out_shape=[jax.ShapeDtypeStruct((n, D_MODEL), F32), jax.ShapeDtypeStruct((n, D_MODEL), BF16)],
        compiler_params=_params(1),
        name=f"mix_out_s{seq_len}",
    )(att, b, y, y, y, sa, sg, h, ada3, ada3, ada3, g_post1, g_pre2, conv_w, w_att_out, w_conv_out, w_o)


FF_CHUNKS = (4 * MXU_N, 4 * MXU_N, 3 * MXU_N)
assert sum(FF_CHUNKS) == D_FF


def _ffn_kernel(tiles_per_seq, u_ref, up_ref, un_ref, h_ref, g2_ref, gpost_ref, cw_ref, wup_ref, wdn_ref, o_ref, ext_ref):
    i = pl.program_id(0)
    tm = u_ref.shape[0]
    has_prev = i % tiles_per_seq != 0
    has_next = i % tiles_per_seq != tiles_per_seq - 1
    zeros = jnp.zeros((HALO, D_MODEL), BF16)
    ext_ref[0:HALO, :] = jnp.where(has_prev, up_ref[...], zeros)
    ext_ref[HALO:HALO + tm, :] = u_ref[...]
    ext_ref[HALO + tm:, :] = jnp.where(has_next, un_ref[...], zeros)
    ext = ext_ref[...]
    rows = tm + 2 * HALO

    def up(lo, width):
        return [_dot(ext, wup_ref[:, col:col + width]) for col in (lo, D_FF + lo)]

    def conv(z, col, width):
        w = cw_ref[:, col:col + width]
        before = pltpu.roll(z, 1, axis=0)[HALO:HALO + tm]
        after = pltpu.roll(z, rows - 1, axis=0)[HALO:HALO + tm]
        return w[0:1] * before + w[1:2] * z[HALO:HALO + tm] + w[2:3] * after

    def down(lo, width, z_gate, z_val):
        gate = conv(z_gate, lo, width)
        val = conv(z_val, D_FF + lo, width)
        act = (gate * jax.nn.sigmoid(gate) * val).astype(BF16)
        half = D_MODEL // 2
        return [_dot(act, wdn_ref[lo:lo + width, n * half:(n + 1) * half]) for n in range(2)]

    starts = [sum(FF_CHUNKS[:c]) for c in range(len(FF_CHUNKS))]
    ff = None
    z_next = up(starts[0], FF_CHUNKS[0])
    for c, (lo, width) in enumerate(zip(starts, FF_CHUNKS)):
        z_cur = z_next
        if c + 1 < len(FF_CHUNKS):
            z_next = up(starts[c + 1], FF_CHUNKS[c + 1])
        parts = down(lo, width, *z_cur)
        ff = parts if ff is None else [a + b for a, b in zip(ff, parts)]
    ff = jnp.concatenate(ff, axis=1)

    o_ref[...] = h_ref[...] + g2_ref[0] * (_rms_rows(ff) * gpost_ref[...])


def _ffn(u2, h1, ada3, ada_row, seq_len, tm, g_post2, conv_ffn, w_up, w_down):
    n = h1.shape[0]
    assert seq_len % tm == 0
    row = lambda i: (i, 0)
    wide = pl.BlockSpec((tm, D_MODEL), row)
    prev, nxt = _halo_specs(tm, n, D_MODEL)
    return pl.pallas_call(
        functools.partial(_ffn_kernel, seq_len // tm),
        grid=(n // tm,),
        in_specs=[wide, prev, nxt, wide, pl.BlockSpec((1, 1, D_MODEL), lambda i: (ada_row(i), 0, 5)),
                  _resident((1, D_MODEL)), _resident((3, 2 * D_FF)),
                  _resident((D_MODEL, 2 * D_FF)), _resident((D_FF, D_MODEL))],
        out_specs=wide,
        out_shape=jax.ShapeDtypeStruct((n, D_MODEL), F32),
        scratch_shapes=[pltpu.VMEM((tm + 2 * HALO, D_MODEL), BF16)],
        compiler_params=_params(1),
        name=f"ffn_s{seq_len}",
    )(u2, u2, u2, h1, ada3, g_post2, conv_ffn, w_up, w_down)


def _rope_tables(n):
    pos = np.arange(n)
    inv = np.power(ROPE_THETA, -np.arange(0, AXIS_DIM, 2, dtype=np.float64) / AXIS_DIM)
    ang_r = (pos // GRID_W)[:, None] * inv[None, :]
    ang_c = (pos % GRID_W)[:, None] * inv[None, :]
    ang = np.concatenate([ang_r, ang_r, ang_c, ang_c], axis=1)
    sign = np.tile(np.concatenate([-np.ones(AXIS_DIM // 2), np.ones(AXIS_DIM // 2)]), 2)
    reps = LANES // HEAD_DIM
    cos = np.tile(np.cos(ang), (1, reps)).astype(np.float32)
    sin = np.tile(np.sin(ang) * sign[None, :], (1, reps)).astype(np.float32)
    return jnp.asarray(cos), jnp.asarray(sin)


def _segment_mean_matrix():
    head = np.arange(MXU_N) // HEAD_DIM
    return jnp.asarray((head[:, None] == head[None, :]).astype(np.float32) / HEAD_DIM, dtype=BF16)


def _layer(x, ada3, ada_row, seq_len, tiles, w, rope_tabs, cache_kv):
    bsz = x.shape[0]
    n = bsz * seq_len
    xf = x.reshape(n, D_MODEL)
    tm_in, tq, sets_per_step, tm_mix, tm_ffn = tiles
    q, kt, v, b, y, sa, sg, *transposed = _in_proj(xf, ada3, lambda i: ada_row(i, tm_in), seq_len, tm_in, w["wq"],
                                                   w["w_in"], w["g_pre1"], w["qg"], w["kg"], w["seg"], rope_tabs)
    q3 = q.reshape(bsz, seq_len, ATT_WIDTH)
    if cache_kv is None:
        att = _attention(q3, kt, v, tq, sets_per_step)
    else:
        ck, cv = cache_kv
        ktc = ck.reshape(bsz, -1, KV_WIDTH).transpose(0, 2, 1).astype(BF16)
        vc = cv.reshape(bsz, -1, KV_WIDTH).astype(BF16)
        att = _attention_pipelined(q3, ktc, kt, vc, v, tq)
    att = att.reshape(n, ATT_WIDTH)
    h1, u2 = _mix_out(att, b, y, sa, sg, xf, ada3, lambda i: ada_row(i, tm_mix), seq_len, tm_mix, w["g_post1"],
                      w["g_pre2"], w["conv_w"], w["w_att_out"], w["w_conv_out"], w["w_o"])
    out = _ffn(u2, h1, ada3, lambda i: ada_row(i, tm_ffn), seq_len, tm_ffn, w["g_post2"], w["conv_ffn"], w["w_up"],
               w["w_down"])
    return (out.reshape(x.shape), *transposed)


def kernel(x_prompt, x_sample, cache_k, cache_v, c, c_ctx, w_ada, b_ada, g_pre1, g_post1, g_pre2, g_post2, w_in, q_norm,
           k_norm, w_att_out, conv_w, w_conv_out, w_o, w_up, conv_ffn, w_down):
    depth = w_in.shape[0]
    dec_batch, dec_seq, _ = x_sample.shape
    batch, seq, _ = x_prompt.shape
    rope_tabs = _rope_tables(dec_seq)
    seg = _segment_mean_matrix()
    cc = jnp.zeros((ADA_ROWS, D_MODEL), F32).at[0].set(c_ctx).at[1:1 + dec_batch].set(c)
    reps = MXU_N // HEAD_DIM

    h_p, h_s = x_prompt, x_sample
    new_ks, new_vs = [], []
    for i in range(depth):
        wq = w_in[i][:, :ATT_WIDTH].reshape(D_MODEL, N_KV_HEADS, GROUP, HEAD_DIM).transpose(0, 2, 1, 3)
        wa = w_att_out[i].reshape(N_KV_HEADS, GROUP, HEAD_DIM, D_MODEL).transpose(1, 0, 2, 3)
        w = {
            "g_pre1": g_pre1[i][None], "g_post1": g_post1[i][None], "g_pre2": g_pre2[i][None], "g_post2": g_post2[i][None],
            "wq": wq.reshape(D_MODEL, ATT_WIDTH).astype(BF16), "w_in": w_in[i].astype(BF16),
            "qg": jnp.tile(q_norm[i] * (HEAD_DIM ** -0.5 * LOG2E), reps)[None],
            "kg": jnp.tile(k_norm[i], reps)[None],
            "seg": seg,
            "w_att_out": wa.reshape(ATT_WIDTH, D_MODEL).astype(BF16), "conv_w": conv_w[i],
            "w_conv_out": w_conv_out[i].astype(BF16),
            "w_o": w_o[i].astype(BF16), "w_up": w_up[i].astype(BF16), "conv_ffn": conv_ffn[i],
            "w_down": w_down[i].astype(BF16),
        }
        ada3 = _ada(cc, w_ada[i], b_ada[i]).reshape(ADA_ROWS, N_ADA, D_MODEL).reshape(ADA_ROWS, 1, N_ADA * D_MODEL)
        h_p, kt_ctx, vt_ctx = _layer(h_p, ada3, lambda t, tm: 0, seq, (512, seq, GROUP, 512, seq), w, None, None)
        new_ks.append(kt_ctx.reshape(batch, N_KV_HEADS, HEAD_DIM, seq).transpose(0, 3, 1, 2))
        new_vs.append(vt_ctx.reshape(batch, N_KV_HEADS, HEAD_DIM, seq).transpose(0, 3, 1, 2))
        (h_s,) = _layer(h_s, ada3, lambda t, tm: 1 + t // (dec_seq // tm), dec_seq, (512, 512, None, 512, 512), w,
                        rope_tabs, (cache_k[:, i], cache_v[:, i]))
    return (h_p, h_s, jnp.stack(new_ks, axis=1), jnp.stack(new_vs, axis=1))
```

```python
import functools

import numpy as np
import jax
import jax.numpy as jnp
from jax.experimental import pallas as pl
from jax.experimental.pallas import tpu as pltpu

D_MODEL = 1024
N_HEADS = 16
N_KV_HEADS = 4
HEAD_DIM = 64
GROUP = N_HEADS // N_KV_HEADS
ATT_WIDTH = N_HEADS * HEAD_DIM
KV_WIDTH = N_KV_HEADS * HEAD_DIM
D_FF = 2816
GRID_W = 64
ROPE_THETA = 10000.0
AXIS_DIM = HEAD_DIM // 2
N_ADA = 6
EPS = 1e-6
LOG2E = 1.4426950408889634

OFF_Q = 0
OFF_K = OFF_Q + ATT_WIDTH
OFF_V = OFF_K + KV_WIDTH
OFF_B = OFF_V + KV_WIDTH
OFF_C = OFF_B + D_MODEL
OFF_X = OFF_C + D_MODEL
OFF_GA = OFF_X + D_MODEL
OFF_GC = OFF_GA + D_MODEL
IN_WIDTH = OFF_GC + D_MODEL

LANES = 128
MXU_N = 256
HALO = 16
ADA_ROWS = 16
VMEM_LIMIT = 56 * 1024 * 1024
KEY_CHUNK = 512

BF16 = jnp.bfloat16
F32 = jnp.float32


def _dot(a, b):
    return jnp.dot(a, b, preferred_element_type=F32)


def _resident(shape):
    nd = len(shape)
    return pl.BlockSpec(shape, lambda *_: (0,) * nd, pipeline_mode=pl.Buffered(1))


def _params(n_axes):
    return pltpu.CompilerParams(dimension_semantics=("arbitrary",) * n_axes, vmem_limit_bytes=VMEM_LIMIT)


def _ada_kernel(c_ref, w_ref, b_ref, o_ref):
    c = c_ref[...]
    s = (c * jax.nn.sigmoid(c)).astype(BF16)
    o_ref[...] = _dot(s, w_ref[...].astype(BF16)) + b_ref[...]


def _ada(cc, w_ada, b_ada):
    n = w_ada.shape[1]
    tn = D_MODEL
    return pl.pallas_call(
        _ada_kernel,
        grid=(n // tn,),
        in_specs=[
            pl.BlockSpec((ADA_ROWS, D_MODEL), lambda j: (0, 0)),
            pl.BlockSpec((D_MODEL, tn), lambda j: (0, j)),
            pl.BlockSpec((1, tn), lambda j: (0, j)),
        ],
        out_specs=pl.BlockSpec((ADA_ROWS, tn), lambda j: (0, j)),
        out_shape=jax.ShapeDtypeStruct((ADA_ROWS, n), F32),
        compiler_params=_params(1),
        name="ada",
    )(cc, w_ada, b_ada.reshape(1, n))


def _rms_rows(x):
    return x * jax.lax.rsqrt(jnp.mean(x * x, axis=-1, keepdims=True) + EPS)


def _head_norm(z, seg_ref, gain):
    ms = _dot((z * z).astype(BF16), seg_ref[...])
    return z * jax.lax.rsqrt(ms + EPS) * gain


def _rope(z, cos, sin, first_half):
    partner = jnp.where(first_half, pltpu.roll(z, LANES - AXIS_DIM // 2, axis=1), pltpu.roll(z, AXIS_DIM // 2, axis=1))
    return z * cos + partner * sin


def _in_proj_kernel(rope, x_ref, sh_ref, sc_ref, gpre_ref, wq_ref, w_ref, qg_ref, kg_ref, seg_ref, *rest):
    if rope:
        cos_ref, sin_ref, q_ref, kt_ref, v_ref, b_ref, y_ref, sa_ref, sg_ref = rest
        cos = cos_ref[...]
        sin = sin_ref[...]
        lane = jax.lax.broadcasted_iota(jnp.int32, (1, LANES), 1)
        first_half = (lane % AXIS_DIM) < (AXIS_DIM // 2)
    else:
        q_ref, kt_ref, v_ref, b_ref, y_ref, sa_ref, sg_ref, ktf_ref, vtf_ref = rest

    x = x_ref[...]
    mod = gpre_ref[...] * (1.0 + sc_ref[0])
    u = (_rms_rows(x) * mod + sh_ref[0]).astype(BF16)

    def rotate(z):
        if not rope:
            return z
        return jnp.concatenate(
            [_rope(z[:, h * LANES:(h + 1) * LANES], cos, sin, first_half) for h in range(MXU_N // LANES)], axis=1)

    def cols(j):
        return slice(j * MXU_N, (j + 1) * MXU_N)

    def store_per_sequence(ref, zt):
        seq = ref.shape[-1]
        for s in range(ref.shape[0]):
            ref[s] = zt[:, s * seq:(s + 1) * seq]

    def store_q(j, z):
        q_ref[:, cols(j)] = rotate(_head_norm(z, seg_ref, qg_ref[...])).astype(BF16)

    def store_k(z):
        k = _head_norm(z, seg_ref, kg_ref[...])
        kt_ref[...] = rotate(k).T.astype(BF16)
        if not rope:
            store_per_sequence(ktf_ref, k.T)

    def store_v(z):
        v_ref[...] = z.astype(BF16)
        if not rope:
            store_per_sequence(vtf_ref, z.T)

    def store_b(j, z):
        b_ref[:, cols(j)] = z.astype(BF16)

    def store_y(j, zc, zx):
        y_ref[:, cols(j)] = (zc * zx).astype(BF16)

    def store_gate(ref, j, z):
        ref[:, cols(j)] = jax.nn.sigmoid(z).astype(BF16)

    work = [(wq_ref, (j * MXU_N,), functools.partial(store_q, j)) for j in range(ATT_WIDTH // MXU_N)]
    work += [(w_ref, (OFF_K,), store_k), (w_ref, (OFF_V,), store_v)]
    for j in range(D_MODEL // MXU_N):
        work += [(w_ref, (OFF_B + j * MXU_N,), functools.partial(store_b, j)),
                 (w_ref, (OFF_C + j * MXU_N, OFF_X + j * MXU_N), functools.partial(store_y, j)),
                 (w_ref, (OFF_GA + j * MXU_N,), functools.partial(store_gate, sa_ref, j)),
                 (w_ref, (OFF_GC + j * MXU_N,), functools.partial(store_gate, sg_ref, j))]
    pending = None
    for weights, offsets, consume in work:
        products = [_dot(u, weights[:, lo:lo + MXU_N]) for lo in offsets]
        if pending is not None:
            pending[0](*pending[1])
        pending = (consume, products)
    pending[0](*pending[1])


def _in_proj(x, ada3, ada_row, seq_len, tm, wq, w_in, g_pre1, qg, kg, seg, rope_tabs):
    n = x.shape[0]
    rope = rope_tabs is not None
    row = lambda i: (i, 0)
    in_specs = [
        pl.BlockSpec((tm, D_MODEL), row),
        pl.BlockSpec((1, 1, D_MODEL), lambda i: (ada_row(i), 0, 0)),
        pl.BlockSpec((1, 1, D_MODEL), lambda i: (ada_row(i), 0, 1)),
        _resident((1, D_MODEL)),
        _resident((D_MODEL, ATT_WIDTH)),
        _resident((D_MODEL, IN_WIDTH)),
        _resident((1, MXU_N)),
        _resident((1, MXU_N)),
        _resident((MXU_N, MXU_N)),
    ]
    args = [x, ada3, ada3, g_pre1, wq, w_in, qg, kg, seg]
    wide = jax.ShapeDtypeStruct((n, D_MODEL), BF16)
    wide_spec = pl.BlockSpec((tm, D_MODEL), row)
    out_shape = [wide, jax.ShapeDtypeStruct((KV_WIDTH, n), BF16), jax.ShapeDtypeStruct((n, KV_WIDTH), BF16)] + [wide] * 4
    out_specs = [wide_spec, pl.BlockSpec((KV_WIDTH, tm), lambda i: (0, i)), pl.BlockSpec((tm, KV_WIDTH), row)]
    out_specs += [wide_spec] * 4
    if rope:
        assert seq_len % tm == 0
        tiles_per_seq = seq_len // tm
        in_specs += [pl.BlockSpec((tm, LANES), lambda i: (i % tiles_per_seq, 0))] * 2
        args += list(rope_tabs)
    else:
        assert tm % seq_len == 0
        per_tile = tm // seq_len
        transposed = jax.ShapeDtypeStruct((n // seq_len, KV_WIDTH, seq_len), F32)
        out_shape += [transposed] * 2
        out_specs += [pl.BlockSpec((per_tile, KV_WIDTH, seq_len), lambda i: (i, 0, 0))] * 2
    return pl.pallas_call(
        functools.partial(_in_proj_kernel, rope),
        grid=(n // tm,),
        in_specs=in_specs,
        out_specs=out_specs,
        out_shape=out_shape,
        compiler_params=_params(1),
        name="in_proj_rope" if rope else "in_proj",
    )(*args)


def _scores(q_ref, kt_ref, col0, g):
    lo = (g // 2) * LANES
    q_pair = q_ref[0, :, col0 + lo:col0 + lo + LANES]
    slot = jax.lax.broadcasted_iota(jnp.int32, (1, LANES), 1) // HEAD_DIM
    q_one = jnp.where(slot == g % 2, q_pair, jnp.zeros_like(q_pair))
    return _dot(q_one, kt_ref[lo:lo + LANES, :])


def _group_of_lane():
    return jax.lax.broadcasted_iota(jnp.int32, (1, MXU_N), 1) // HEAD_DIM


def _sum_lane(g):
    return (HEAD_DIM * (g + 1)) % MXU_N


def _build_masked_v(pieces, vm_ref):
    lane = jax.lax.broadcasted_iota(jnp.int32, (1, MXU_N), 1)
    for row0, v in pieces:
        v = v.astype(F32)
        for g in range(N_KV_HEADS):
            ones_col = (lane == _sum_lane(g)).astype(F32)
            vm_ref[g, row0:row0 + v.shape[0], :] = jnp.where(_group_of_lane() == g, v, ones_col).astype(BF16)


def _unnormalised_probs(s):
    return jnp.exp2(s - jnp.max(s, axis=-1, keepdims=True)).astype(BF16)


def _weighted_values(probs, vm_ref):
    acc = None
    den = None
    for g in range(N_KV_HEADS):
        pv = _dot(probs[g], vm_ref[g])
        own = _group_of_lane() == g
        row_sum = pv[:, _sum_lane(g):_sum_lane(g) + 1]
        acc = pv if acc is None else jnp.where(own, pv, acc)
        den = row_sum if den is None else jnp.where(own, row_sum, den)
    return acc / den


def _attn_kernel(n_sets, q_ref, kt_ref, v_ref, o_ref, vm_ref):
    @pl.when((pl.program_id(1) == 0) & (pl.program_id(2) == 0))
    def _():
        _build_masked_v([(0, v_ref[...])], vm_ref)

    def probs_of(j):
        return [_unnormalised_probs(_scores(q_ref, kt_ref, j * MXU_N, g)) for g in range(N_KV_HEADS)]

    probs_next = probs_of(0)
    for j in range(n_sets):
        probs = probs_next
        if j + 1 < n_sets:
            probs_next = probs_of(j + 1)
        o_ref[0, :, j * MXU_N:(j + 1) * MXU_N] = _weighted_values(probs, vm_ref).astype(BF16)


def _attention(q, kt, v, tq, sets_per_step):
    bsz, n, _ = q.shape
    t = kt.shape[-1] // bsz
    width = sets_per_step * MXU_N
    return pl.pallas_call(
        functools.partial(_attn_kernel, sets_per_step),
        grid=(bsz, GROUP // sets_per_step, n // tq),
        in_specs=[
            pl.BlockSpec((1, tq, width), lambda b, j, i: (b, i, j)),
            pl.BlockSpec((KV_WIDTH, t), lambda b, j, i: (0, b)),
            pl.BlockSpec((t, KV_WIDTH), lambda b, j, i: (b, 0)),
        ],
        out_specs=pl.BlockSpec((1, tq, width), lambda b, j, i: (b, i, j)),
        out_shape=jax.ShapeDtypeStruct(q.shape, BF16),
        scratch_shapes=[pltpu.VMEM((N_KV_HEADS, t, MXU_N), BF16)],
        compiler_params=_params(3),
        name=f"attn_t{t}",
    )(q, kt, v)


def _attn_pipe_kernel(steps_per_batch, q_ref, ktc_ref, ktn_ref, vc_ref, vn_ref, o_ref, vm_ref, pa_ref, pb_ref, s_ref):
    t = pl.program_id(0)

    @pl.when(t == 0)
    def _():
        pb_ref[...] = jnp.ones_like(pb_ref)

    @pl.when((t == 0) | ((t - 1) % steps_per_batch == 0))
    def _():
        _build_masked_v([(0, vc_ref[0]), (vc_ref.shape[1], vn_ref[...])], vm_ref)

    cached_chunks = ktc_ref.shape[-1] // KEY_CHUNK
    n_chunks = cached_chunks + ktn_ref.shape[-1] // KEY_CHUNK
    slot_of_lane = jax.lax.broadcasted_iota(jnp.int32, (1, LANES), 1) // HEAD_DIM

    def keys(c):
        return slice(c * KEY_CHUNK, (c + 1) * KEY_CHUNK)

    def score_chunk(g, c, run_max):
        lo = (g // 2) * LANES
        q_pair = q_ref[0, :, lo:lo + LANES]
        q_one = jnp.where(slot_of_lane == g % 2, q_pair, jnp.zeros_like(q_pair))
        if c < cached_chunks:
            kt = ktc_ref[0, lo:lo + LANES, keys(c)]
        else:
            kt = ktn_ref[lo:lo + LANES, keys(c - cached_chunks)]
        s = _dot(q_one, kt)
        s_ref[g % 2, :, keys(c)] = s
        for k in range(KEY_CHUNK // LANES):
            part = s[:, k * LANES:(k + 1) * LANES]
            run_max = part if run_max is None else jnp.maximum(run_max, part)
        return run_max

    def step(p_new, p_old):
        run_max = None
        for c in range(n_chunks):
            run_max = score_chunk(0, c, run_max)
        out = None
        den = None
        for g in range(N_KV_HEADS):
            row_max = jnp.max(run_max, axis=-1, keepdims=True)
            run_max = None
            acc = None
            for c in range(n_chunks):
                p_new[g, :, keys(c)] = jnp.exp2(s_ref[g % 2, :, keys(c)] - row_max).astype(BF16)
                if g + 1 < N_KV_HEADS:
                    run_max = score_chunk(g + 1, c, run_max)
                pv = _dot(p_old[g, :, keys(c)], vm_ref[g, keys(c), :])
                acc = pv if acc is None else acc + pv
            own = _group_of_lane() == g
            row_sum = acc[:, _sum_lane(g):_sum_lane(g) + 1]
            out = acc if out is None else jnp.where(own, acc, out)
            den = row_sum if den is None else jnp.where(own, row_sum, den)
        o_ref[0] = (out / den).astype(BF16)

    @pl.when(t % 2 == 0)
    def _():
        step(pa_ref, pb_ref)

    @pl.when(t % 2 == 1)
    def _():
        step(pb_ref, pa_ref)


def _attention_pipelined(q, ktc, ktn, vc, vn, tq):
    bsz, n, _ = q.shape
    t_cached = ktc.shape[-1]
    assert t_cached % KEY_CHUNK == 0 and n % KEY_CHUNK == 0
    t_keys = t_cached + n
    tiles = n // tq
    steps_per_batch = GROUP * tiles
    n_items = bsz * steps_per_batch

    def item(t):
        return t // steps_per_batch, (t // tiles) % GROUP, t % tiles

    def cur(t):
        return item(jnp.minimum(t, n_items - 1))

    def prev(t):
        return item(jnp.clip(t - 1, 0, n_items - 1))

    def q_map(t):
        b, j, i = cur(t)
        return b, i, j

    def o_map(t):
        b, j, i = prev(t)
        return b, i, j

    return pl.pallas_call(
        functools.partial(_attn_pipe_kernel, steps_per_batch),
        grid=(n_items + 1,),
        in_specs=[
            pl.BlockSpec((1, tq, MXU_N), q_map),
            pl.BlockSpec((1, KV_WIDTH, t_cached), lambda t: (cur(t)[0], 0, 0)),
            pl.BlockSpec((KV_WIDTH, n), lambda t: (0, cur(t)[0])),
            pl.BlockSpec((1, t_cached, KV_WIDTH), lambda t: (prev(t)[0], 0, 0)),
            pl.BlockSpec((n, KV_WIDTH), lambda t: (prev(t)[0], 0)),
        ],
        out_specs=pl.BlockSpec((1, tq, MXU_N), o_map),
        out_shape=jax.ShapeDtypeStruct(q.shape, BF16),
        scratch_shapes=[pltpu.VMEM((N_KV_HEADS, t_keys, MXU_N), BF16),
                        pltpu.VMEM((N_KV_HEADS, tq, t_keys), BF16), pltpu.VMEM((N_KV_HEADS, tq, t_keys), BF16),
                        pltpu.VMEM((2, tq, t_keys), F32)],
        compiler_params=_params(1),
        name=f"attn_pipe_t{t_keys}",
    )(q, ktc, ktn, vc, vn)


def _shifted_rows(x, prev_row, next_row, period):
    tm = x.shape[0]
    t = jax.lax.broadcasted_iota(jnp.int32, (tm, 1), 0) % period
    before = jnp.where(t == 0, prev_row, pltpu.roll(x, 1, axis=0))
    after = jnp.where(t == period - 1, next_row, pltpu.roll(x, tm - 1, axis=0))
    return before, after


def _mix_out_kernel(tiles_per_seq, period, att_ref, b_ref, y_ref, yp_ref, yn_ref, sa_ref, sg_ref, h_ref, g1_ref, sh2_ref,
                    sc2_ref, gpost_ref, gpre_ref, cw_ref, wa_ref, wc_ref, wo_ref, h1_ref, u2_ref):
    i = pl.program_id(0)
    tm = y_ref.shape[0]
    halves = [slice(0, tm // 2), slice(tm // 2, tm)]
    has_prev = (i % tiles_per_seq != 0).astype(F32)
    has_next = (i % tiles_per_seq != tiles_per_seq - 1).astype(F32)
    y = y_ref[...].astype(F32)
    y_prev = yp_ref[HALO - 1:HALO, :].astype(F32) * has_prev
    y_next = yn_ref[0:1, :].astype(F32) * has_next
    before, after = _shifted_rows(y, y_prev, y_next, period)
    conv = cw_ref[0:1, :] * before + cw_ref[1:2, :] * y + cw_ref[2:3, :] * after
    conv_in = (b_ref[...].astype(F32) * conv).astype(BF16)
    cnv = [_dot(conv_in[r, :], wc_ref[...]) for r in halves]
    att = [_dot(att_ref[r, :], wa_ref[...]) for r in halves]

    mixed = []
    half = D_MODEL // 2
    for r, a, c in zip(halves, att, cnv):
        merged = (sa_ref[r, :].astype(F32) * a + sg_ref[r, :].astype(F32) * c).astype(BF16)
        mixed.append(jnp.concatenate([_dot(merged, wo_ref[:, n * half:(n + 1) * half]) for n in range(2)], axis=1))

    for r, mo in zip(halves, mixed):
        h1 = h_ref[r, :] + g1_ref[0] * (_rms_rows(mo) * gpost_ref[...])
        h1_ref[r, :] = h1
        u2 = _rms_rows(h1) * (gpre_ref[...] * (1.0 + sc2_ref[0])) + sh2_ref[0]
        u2_ref[r, :] = u2.astype(BF16)


def _halo_specs(tm, n_rows, width):
    per = tm // HALO
    last = n_rows // HALO - 1
    prev = pl.BlockSpec((HALO, width), lambda i: (jnp.maximum(i * per - 1, 0), 0))
    nxt = pl.BlockSpec((HALO, width), lambda i: (jnp.minimum((i + 1) * per, last), 0))
    return prev, nxt


def _mix_out(att, b, y, sa, sg, h, ada3, ada_row, seq_len, tm, g_post1, g_pre2, conv_w, w_att_out, w_conv_out, w_o):
    n = h.shape[0]
    assert seq_len % tm == 0 or tm % seq_len == 0
    tiles_per_seq = max(seq_len // tm, 1)
    row = lambda i: (i, 0)
    wide = pl.BlockSpec((tm, D_MODEL), row)
    prev, nxt = _halo_specs(tm, n, D_MODEL)
    ada = lambda k: pl.BlockSpec((1, 1, D_MODEL), lambda i: (ada_row(i), 0, k))
    sq = (D_MODEL, D_MODEL)
    return pl.pallas_call(
        functools.partial(_mix_out_kernel, tiles_per_seq, min(seq_len, tm)),
        grid=(n // tm,),
        in_specs=[wide, wide, wide, prev, nxt, wide, wide, wide, ada(2), ada(3), ada(4),
                  _resident((1, D_MODEL)), _resident((1, D_MODEL)), _resident((3, D_MODEL)),
                  _resident(sq), _resident(sq), _resident(sq)],
        out_specs=[wide, wide],
        out_shape=[jax.ShapeDtypeStruct((n, D_MODEL), F32), jax.ShapeDtypeStruct((n, D_MODEL), BF16)],
        compiler_params=_params(1),
        name=f"mix_out_s{seq_len}",
    )(att, b, y, y, y, sa, sg, h, ada3, ada3, ada3, g_post1, g_pre2, conv_w, w_att_out, w_conv_out, w_o)


FF_CHUNKS = (4 * MXU_N, 4 * MXU_N, 3 * MXU_N)
assert sum(FF_CHUNKS) == D_FF


def _ffn_kernel(tiles_per_seq, u_ref, up_ref, un_ref, h_ref, g2_ref, gpost_ref, cw_ref, wup_ref, wdn_ref, o_ref, ext_ref):
    i = pl.program_id(0)
    tm = u_ref.shape[0]
    has_prev = i % tiles_per_seq != 0
    has_next = i % tiles_per_seq != tiles_per_seq - 1
    zeros = jnp.zeros((HALO, D_MODEL), BF16)
    ext_ref[0:HALO, :] = jnp.where(has_prev, up_ref[...], zeros)
    ext_ref[HALO:HALO + tm, :] = u_ref[...]
    ext_ref[HALO + tm:, :] = jnp.where(has_next, un_ref[...], zeros)
    ext = ext_ref[...]
    rows = tm + 2 * HALO

    def up(lo, width):
        return [_dot(ext, wup_ref[:, col:col + width]) for col in (lo, D_FF + lo)]

    def conv(z, col, width):
        w = cw_ref[:, col:col + width]
        before = pltpu.roll(z, 1, axis=0)[HALO:HALO + tm]
        after = pltpu.roll(z, rows - 1, axis=0)[HALO:HALO + tm]
        return w[0:1] * before + w[1:2] * z[HALO:HALO + tm] + w[2:3] * after

    def down(lo, width, z_gate, z_val):
        gate = conv(z_gate, lo, width)
        val = conv(z_val, D_FF + lo, width)
        act = (gate * jax.nn.sigmoid(gate) * val).astype(BF16)
        half = D_MODEL // 2
        return [_dot(act, wdn_ref[lo:lo + width, n * half:(n + 1) * half]) for n in range(2)]

    starts = [sum(FF_CHUNKS[:c]) for c in range(len(FF_CHUNKS))]
    ff = None
    z_next = up(starts[0], FF_CHUNKS[0])
    for c, (lo, width) in enumerate(zip(starts, FF_CHUNKS)):
        z_cur = z_next
        if c + 1 < len(FF_CHUNKS):
            z_next = up(starts[c + 1], FF_CHUNKS[c + 1])
        parts = down(lo, width, *z_cur)
        ff = parts if ff is None else [a + b for a, b in zip(ff, parts)]
    ff = jnp.concatenate(ff, axis=1)

    o_ref[...] = h_ref[...] + g2_ref[0] * (_rms_rows(ff) * gpost_ref[...])


def _ffn(u2, h1, ada3, ada_row, seq_len, tm, g_post2, conv_ffn, w_up, w_down):
    n = h1.shape[0]
    assert seq_len % tm == 0
    row = lambda i: (i, 0)
    wide = pl.BlockSpec((tm, D_MODEL), row)
    prev, nxt = _halo_specs(tm, n, D_MODEL)
    return pl.pallas_call(
        functools.partial(_ffn_kernel, seq_len // tm),
        grid=(n // tm,),
        in_specs=[wide, prev, nxt, wide, pl.BlockSpec((1, 1, D_MODEL), lambda i: (ada_row(i), 0, 5)),
                  _resident((1, D_MODEL)), _resident((3, 2 * D_FF)),
                  _resident((D_MODEL, 2 * D_FF)), _resident((D_FF, D_MODEL))],
        out_specs=wide,
        out_shape=jax.ShapeDtypeStruct((n, D_MODEL), F32),
        scratch_shapes=[pltpu.VMEM((tm + 2 * HALO, D_MODEL), BF16)],
        compiler_params=_params(1),
        name=f"ffn_s{seq_len}",
    )(u2, u2, u2, h1, ada3, g_post2, conv_ffn, w_up, w_down)


def _rope_tables(n):
    pos = np.arange(n)
    inv = np.power(ROPE_THETA, -np.arange(0, AXIS_DIM, 2, dtype=np.float64) / AXIS_DIM)
    ang_r = (pos // GRID_W)[:, None] * inv[None, :]
    ang_c = (pos % GRID_W)[:, None] * inv[None, :]
    ang = np.concatenate([ang_r, ang_r, ang_c, ang_c], axis=1)
    sign = np.tile(np.concatenate([-np.ones(AXIS_DIM // 2), np.ones(AXIS_DIM // 2)]), 2)
    reps = LANES // HEAD_DIM
    cos = np.tile(np.cos(ang), (1, reps)).astype(np.float32)
    sin = np.tile(np.sin(ang) * sign[None, :], (1, reps)).astype(np.float32)
    return jnp.asarray(cos), jnp.asarray(sin)


def _segment_mean_matrix():
    head = np.arange(MXU_N) // HEAD_DIM
    return jnp.asarray((head[:, None] == head[None, :]).astype(np.float32) / HEAD_DIM, dtype=BF16)


def _layer(x, ada3, ada_row, seq_len, tiles, w, rope_tabs, cache_kv):
    bsz = x.shape[0]
    n = bsz * seq_len
    xf = x.reshape(n, D_MODEL)
    tm_in, tq, sets_per_step, tm_mix, tm_ffn = tiles
    q, kt, v, b, y, sa, sg, *transposed = _in_proj(xf, ada3, lambda i: ada_row(i, tm_in), seq_len, tm_in, w["wq"],
                                                   w["w_in"], w["g_pre1"], w["qg"], w["kg"], w["seg"], rope_tabs)
    q3 = q.reshape(bsz, seq_len, ATT_WIDTH)
    if cache_kv is None:
        att = _attention(q3, kt, v, tq, sets_per_step)
    else:
        ck, cv = cache_kv
        ktc = ck.reshape(bsz, -1, KV_WIDTH).transpose(0, 2, 1).astype(BF16)
        vc = cv.reshape(bsz, -1, KV_WIDTH).astype(BF16)
        att = _attention_pipelined(q3, ktc, kt, vc, v, tq)
    att = att.reshape(n, ATT_WIDTH)
    h1, u2 = _mix_out(att, b, y, sa, sg, xf, ada3, lambda i: ada_row(i, tm_mix), seq_len, tm_mix, w["g_post1"],
                      w["g_pre2"], w["conv_w"], w["w_att_out"], w["w_conv_out"], w["w_o"])
    out = _ffn(u2, h1, ada3, lambda i: ada_row(i, tm_ffn), seq_len, tm_ffn, w["g_post2"], w["conv_ffn"], w["w_up"],
               w["w_down"])
    return (out.reshape(x.shape), *transposed)


def kernel(x_prompt, x_sample, cache_k, cache_v, c, c_ctx, w_ada, b_ada, g_pre1, g_post1, g_pre2, g_post2, w_in, q_norm,
           k_norm, w_att_out, conv_w, w_conv_out, w_o, w_up, conv_ffn, w_down):
    depth = w_in.shape[0]
    dec_batch, dec_seq, _ = x_sample.shape
    batch, seq, _ = x_prompt.shape
    rope_tabs = _rope_tables(dec_seq)
    seg = _segment_mean_matrix()
    cc = jnp.zeros((ADA_ROWS, D_MODEL), F32).at[0].set(c_ctx).at[1:1 + dec_batch].set(c)
    reps = MXU_N // HEAD_DIM

    h_p, h_s = x_prompt, x_sample
    new_ks, new_vs = [], []
    for i in range(depth):
        wq = w_in[i][:, :ATT_WIDTH].reshape(D_MODEL, N_KV_HEADS, GROUP, HEAD_DIM).transpose(0, 2, 1, 3)
        wa = w_att_out[i].reshape(N_KV_HEADS, GROUP, HEAD_DIM, D_MODEL).transpose(1, 0, 2, 3)
        w = {
            "g_pre1": g_pre1[i][None], "g_post1": g_post1[i][None], "g_pre2": g_pre2[i][None], "g_post2": g_post2[i][None],
            "wq": wq.reshape(D_MODEL, ATT_WIDTH).astype(BF16), "w_in": w_in[i].astype(BF16),
            "qg": jnp.tile(q_norm[i] * (HEAD_DIM ** -0.5 * LOG2E), reps)[None],
            "kg": jnp.tile(k_norm[i], reps)[None],
            "seg": seg,
            "w_att_out": wa.reshape(ATT_WIDTH, D_MODEL).astype(BF16), "conv_w": conv_w[i],
            "w_conv_out": w_conv_out[i].astype(BF16),
            "w_o": w_o[i].astype(BF16), "w_up": w_up[i].astype(BF16), "conv_ffn": conv_ffn[i],
            "w_down": w_down[i].astype(BF16),
        }
        ada3 = _ada(cc, w_ada[i], b_ada[i]).reshape(ADA_ROWS, N_ADA, D_MODEL).reshape(ADA_ROWS, 1, N_ADA * D_MODEL)
        h_p, kt_ctx, vt_ctx = _layer(h_p, ada3, lambda t, tm: 0, seq, (512, seq, GROUP, 512, seq), w, None, None)
        new_ks.append(kt_ctx.reshape(batch, N_KV_HEADS, HEAD_DIM, seq).transpose(0, 3, 1, 2))
        new_vs.append(vt_ctx.reshape(batch, N_KV_HEADS, HEAD_DIM, seq).transpose(0, 3, 1, 2))
        (h_s,) = _layer(h_s, ada3, lambda t, tm: 1 + t // (dec_seq // tm), dec_seq, (512, 512, None, 512, 512), w,
                        rope_tabs, (cache_k[:, i], cache_v[:, i]))
    return (h_p, h_s, jnp.stack(new_ks, axis=1), jnp.stack(new_vs, axis=1))
```

```python
import functools

import numpy as np
import jax
import jax.numpy as jnp
from jax.experimental import pallas as pl
from jax.experimental.pallas import tpu as pltpu

D_MODEL = 1024
N_HEADS = 16
N_KV_HEADS = 4
HEAD_DIM = 64
GROUP = N_HEADS // N_KV_HEADS
ATT_WIDTH = N_HEADS * HEAD_DIM
KV_WIDTH = N_KV_HEADS * HEAD_DIM
D_FF = 2816
GRID_W = 64
ROPE_THETA = 10000.0
AXIS_DIM = HEAD_DIM // 2
N_ADA = 6
EPS = 1e-6
LOG2E = 1.4426950408889634

OFF_Q = 0
OFF_K = OFF_Q + ATT_WIDTH
OFF_V = OFF_K + KV_WIDTH
OFF_B = OFF_V + KV_WIDTH
OFF_C = OFF_B + D_MODEL
OFF_X = OFF_C + D_MODEL
OFF_GA = OFF_X + D_MODEL
OFF_GC = OFF_GA + D_MODEL
IN_WIDTH = OFF_GC + D_MODEL

LANES = 128
MXU_N = 256
HALO = 16
ADA_ROWS = 16
VMEM_LIMIT = 56 * 1024 * 1024
KEY_CHUNK = 512

BF16 = jnp.bfloat16
F32 = jnp.float32


def _dot(a, b):
    return jnp.dot(a, b, preferred_element_type=F32)


def _resident(shape):
    nd = len(shape)
    return pl.BlockSpec(shape, lambda *_: (0,) * nd, pipeline_mode=pl.Buffered(1))


def _params(n_axes):
    return pltpu.CompilerParams(dimension_semantics=("arbitrary",) * n_axes, vmem_limit_bytes=VMEM_LIMIT)


def _ada_kernel(c_ref, w_ref, b_ref, o_ref):
    c = c_ref[...]
    s = (c * jax.nn.sigmoid(c)).astype(BF16)
    o_ref[...] = _dot(s, w_ref[...].astype(BF16)) + b_ref[...]


def _ada(cc, w_ada, b_ada):
    n = w_ada.shape[1]
    tn = D_MODEL
    return pl.pallas_call(
        _ada_kernel,
        grid=(n // tn,),
        in_specs=[
            pl.BlockSpec((ADA_ROWS, D_MODEL), lambda j: (0, 0)),
            pl.BlockSpec((D_MODEL, tn), lambda j: (0, j)),
            pl.BlockSpec((1, tn), lambda j: (0, j)),
        ],
        out_specs=pl.BlockSpec((ADA_ROWS, tn), lambda j: (0, j)),
        out_shape=jax.ShapeDtypeStruct((ADA_ROWS, n), F32),
        compiler_params=_params(1),
        name="ada",
    )(cc, w_ada, b_ada.reshape(1, n))


def _rms_rows(x):
    return x * jax.lax.rsqrt(jnp.mean(x * x, axis=-1, keepdims=True) + EPS)


def _head_norm(z, seg_ref, gain):
    ms = _dot((z * z).astype(BF16), seg_ref[...])
    return z * jax.lax.rsqrt(ms + EPS) * gain


def _rope(z, cos, sin, first_half):
    partner = jnp.where(first_half, pltpu.roll(z, LANES - AXIS_DIM // 2, axis=1), pltpu.roll(z, AXIS_DIM // 2, axis=1))
    return z * cos + partner * sin


def _in_proj_kernel(rope, tiles_per_seq, period, x_ref, xp_ref, xn_ref, sh_ref, sc_ref, gpre_ref, wq_ref, w_ref, qg_ref,
                    kg_ref, seg_ref, cw_ref, *rest):
    if rope:
        cos_ref, sin_ref, q_ref, kt_ref, v_ref, cin_ref, sa_ref, sg_ref, ext_ref = rest
        cos = cos_ref[...]
        sin = sin_ref[...]
        lane = jax.lax.broadcasted_iota(jnp.int32, (1, LANES), 1)
        first_half = (lane % AXIS_DIM) < (AXIS_DIM // 2)
    else:
        q_ref, kt_ref, v_ref, cin_ref, sa_ref, sg_ref, ktf_ref, vtf_ref, ext_ref = rest

    i = pl.program_id(0)
    tm = x_ref.shape[0]
    rows = tm + 2 * HALO
    mod = gpre_ref[...] * (1.0 + sc_ref[0])

    def pre_norm(x):
        return (_rms_rows(x) * mod + sh_ref[0]).astype(BF16)

    has_prev = i % tiles_per_seq != 0
    has_next = i % tiles_per_seq != tiles_per_seq - 1
    zeros = jnp.zeros((HALO, D_MODEL), BF16)
    ext_ref[0:HALO, :] = jnp.where(has_prev, pre_norm(xp_ref[...]), zeros)
    ext_ref[HALO:HALO + tm, :] = pre_norm(x_ref[...])
    ext_ref[HALO + tm:, :] = jnp.where(has_next, pre_norm(xn_ref[...]), zeros)
    u = ext_ref[HALO:HALO + tm, :]
    u_ext = ext_ref[...]

    def rotate(z):
        if not rope:
            return z
        return jnp.concatenate(
            [_rope(z[:, h * LANES:(h + 1) * LANES], cos, sin, first_half) for h in range(MXU_N // LANES)], axis=1)

    def cols(j):
        return slice(j * MXU_N, (j + 1) * MXU_N)

    def store_per_sequence(ref, zt):
        seq = ref.shape[-1]
        for s in range(ref.shape[0]):
            ref[s] = zt[:, s * seq:(s + 1) * seq]

    def store_q(j, z):
        q_ref[:, cols(j)] = rotate(_head_norm(z, seg_ref, qg_ref[...])).astype(BF16)

    def store_k(z):
        k = _head_norm(z, seg_ref, kg_ref[...])
        kt_ref[...] = rotate(k).T.astype(BF16)
        if not rope:
            store_per_sequence(ktf_ref, k.T)

    def store_v(z):
        v_ref[...] = z.astype(BF16)
        if not rope:
            store_per_sequence(vtf_ref, z.T)

    def store_conv_in(j, zb, zc, zx):
        w = cw_ref[:, cols(j)]
        y = zc * zx
        before = pltpu.roll(y, 1, axis=0)[HALO:HALO + tm]
        after = pltpu.roll(y, rows - 1, axis=0)[HALO:HALO + tm]
        if period < tm:
            t = jax.lax.broadcasted_iota(jnp.int32, (tm, 1), 0) % period
            before = jnp.where(t == 0, 0.0, before)
            after = jnp.where(t == period - 1, 0.0, after)
        conv = w[0:1] * before + w[1:2] * y[HALO:HALO + tm] + w[2:3] * after
        cin_ref[:, cols(j)] = (zb * conv).astype(BF16)

    def store_gate(ref, j, z):
        ref[:, cols(j)] = jax.nn.sigmoid(z).astype(BF16)

    work = [([(u, wq_ref, j * MXU_N)], functools.partial(store_q, j)) for j in range(ATT_WIDTH // MXU_N)]
    work += [([(u, w_ref, OFF_K)], store_k)]
    for j in range(D_MODEL // MXU_N):
        conv_operands = [(u, w_ref, OFF_B + j * MXU_N), (u_ext, w_ref, OFF_C + j * MXU_N), (u_ext, w_ref, OFF_X + j * MXU_N)]
        work += [(conv_operands, functools.partial(store_conv_in, j)),
                 ([(u, w_ref, OFF_GA + j * MXU_N)], functools.partial(store_gate, sa_ref, j)),
                 ([(u, w_ref, OFF_GC + j * MXU_N)], functools.partial(store_gate, sg_ref, j))]
    work += [([(u, w_ref, OFF_V)], store_v)]
    pending = None
    for operands, consume in work:
        products = [_dot(lhs, weights[:, lo:lo + MXU_N]) for lhs, weights, lo in operands]
        if pending is not None:
            pending[0](*pending[1])
        pending = (consume, products)
    pending[0](*pending[1])


def _in_proj(x, ada3, ada_row, seq_len, tm, wq, w_in, g_pre1, qg, kg, seg, conv_w, rope_tabs):
    n = x.shape[0]
    rope = rope_tabs is not None
    assert seq_len % tm == 0 or tm % seq_len == 0
    tiles_per_seq = max(seq_len // tm, 1)
    row = lambda i: (i, 0)
    prev, nxt = _halo_specs(tm, n, D_MODEL)
    in_specs = [
        pl.BlockSpec((tm, D_MODEL), row), prev, nxt,
        pl.BlockSpec((1, 1, D_MODEL), lambda i: (ada_row(i), 0, 0)),
        pl.BlockSpec((1, 1, D_MODEL), lambda i: (ada_row(i), 0, 1)),
        _resident((1, D_MODEL)),
        _resident((D_MODEL, ATT_WIDTH)),
        _resident((D_MODEL, IN_WIDTH)),
        _resident((1, MXU_N)),
        _resident((1, MXU_N)),
        _resident((MXU_N, MXU_N)),
        _resident((3, D_MODEL)),
    ]
    args = [x, x, x, ada3, ada3, g_pre1, wq, w_in, qg, kg, seg, conv_w]
    wide = jax.ShapeDtypeStruct((n, D_MODEL), BF16)
    wide_spec = pl.BlockSpec((tm, D_MODEL), row)
    out_shape = [wide, jax.ShapeDtypeStruct((KV_WIDTH, n), BF16), jax.ShapeDtypeStruct((n, KV_WIDTH), BF16)] + [wide] * 3
    out_specs = [wide_spec, pl.BlockSpec((KV_WIDTH, tm), lambda i: (0, i)), pl.BlockSpec((tm, KV_WIDTH), row)]
    out_specs += [wide_spec] * 3
    if rope:
        assert seq_len % tm == 0
        in_specs += [pl.BlockSpec((tm, LANES), lambda i: (i % tiles_per_seq, 0))] * 2
        args += list(rope_tabs)
    else:
        assert tm % seq_len == 0
        per_tile = tm // seq_len
        transposed = jax.ShapeDtypeStruct((n // seq_len, KV_WIDTH, seq_len), F32)
        out_shape += [transposed] * 2
        out_specs += [pl.BlockSpec((per_tile, KV_WIDTH, seq_len), lambda i: (i, 0, 0))] * 2
    return pl.pallas_call(
        functools.partial(_in_proj_kernel, rope, tiles_per_seq, min(seq_len, tm)),
        grid=(n // tm,),
        in_specs=in_specs,
        out_specs=out_specs,
        out_shape=out_shape,
        scratch_shapes=[pltpu.VMEM((tm + 2 * HALO, D_MODEL), BF16)],
        compiler_params=_params(1),
        name="in_proj_rope" if rope else "in_proj",
    )(*args)


def _scores(q_ref, kt_ref, col0, g):
    lo = (g // 2) * LANES
    q_pair = q_ref[0, :, col0 + lo:col0 + lo + LANES]
    slot = jax.lax.broadcasted_iota(jnp.int32, (1, LANES), 1) // HEAD_DIM
    q_one = jnp.where(slot == g % 2, q_pair, jnp.zeros_like(q_pair))
    return _dot(q_one, kt_ref[lo:lo + LANES, :])


def _group_of_lane():
    return jax.lax.broadcasted_iota(jnp.int32, (1, MXU_N), 1) // HEAD_DIM


def _sum_lane(g):
    return (HEAD_DIM * (g + 1)) % MXU_N


def _build_masked_v(pieces, vm_ref):
    lane = jax.lax.broadcasted_iota(jnp.int32, (1, MXU_N), 1)
    for row0, v in pieces:
        v = v.astype(F32)
        for g in range(N_KV_HEADS):
            ones_col = (lane == _sum_lane(g)).astype(F32)
            vm_ref[g, row0:row0 + v.shape[0], :] = jnp.where(_group_of_lane() == g, v, ones_col).astype(BF16)


def _unnormalised_probs(s):
    return jnp.exp2(s - jnp.max(s, axis=-1, keepdims=True)).astype(BF16)


def _weighted_values(probs, vm_ref):
    acc = None
    den = None
    for g in range(N_KV_HEADS):
        pv = _dot(probs[g], vm_ref[g])
        own = _group_of_lane() == g
        row_sum = pv[:, _sum_lane(g):_sum_lane(g) + 1]
        acc = pv if acc is None else jnp.where(own, pv, acc)
        den = row_sum if den is None else jnp.where(own, row_sum, den)
    return acc / den


def _attn_kernel(n_sets, q_ref, kt_ref, v_ref, o_ref, vm_ref):
    @pl.when((pl.program_id(1) == 0) & (pl.program_id(2) == 0))
    def _():
        _build_masked_v([(0, v_ref[...])], vm_ref)

    def probs_of(j):
        return [_unnormalised_probs(_scores(q_ref, kt_ref, j * MXU_N, g)) for g in range(N_KV_HEADS)]

    probs_next = probs_of(0)
    for j in range(n_sets):
        probs = probs_next
        if j + 1 < n_sets:
            probs_next = probs_of(j + 1)
        o_ref[0, :, j * MXU_N:(j + 1) * MXU_N] = _weighted_values(probs, vm_ref).astype(BF16)


def _attention(q, kt, v, tq, sets_per_step):
    bsz, n, _ = q.shape
    t = kt.shape[-1] // bsz
    width = sets_per_step * MXU_N
    return pl.pallas_call(
        functools.partial(_attn_kernel, sets_per_step),
        grid=(bsz, GROUP // sets_per_step, n // tq),
        in_specs=[
            pl.BlockSpec((1, tq, width), lambda b, j, i: (b, i, j)),
            pl.BlockSpec((KV_WIDTH, t), lambda b, j, i: (0, b)),
            pl.BlockSpec((t, KV_WIDTH), lambda b, j, i: (b, 0)),
        ],
        out_specs=pl.BlockSpec((1, tq, width), lambda b, j, i: (b, i, j)),
        out_shape=jax.ShapeDtypeStruct(q.shape, BF16),
        scratch_shapes=[pltpu.VMEM((N_KV_HEADS, t, MXU_N), BF16)],
        compiler_params=_params(3),
        name=f"attn_t{t}",
    )(q, kt, v)


def _attn_pipe_kernel(steps_per_batch, q_ref, ktc_ref, ktn_ref, vc_ref, vn_ref, o_ref, vm_ref, pa_ref, pb_ref, s_ref):
    t = pl.program_id(0)

    @pl.when(t == 0)
    def _():
        pb_ref[...] = jnp.ones_like(pb_ref)

    @pl.when((t == 0) | ((t - 1) % steps_per_batch == 0))
    def _():
        _build_masked_v([(0, vc_ref[0]), (vc_ref.shape[1], vn_ref[...])], vm_ref)

    cached_chunks = ktc_ref.shape[-1] // KEY_CHUNK
    n_chunks = cached_chunks + ktn_ref.shape[-1] // KEY_CHUNK
    slot_of_lane = jax.lax.broadcasted_iota(jnp.int32, (1, LANES), 1) // HEAD_DIM

    def keys(c):
        return slice(c * KEY_CHUNK, (c + 1) * KEY_CHUNK)

    def score_chunk(g, c, run_max):
        lo = (g // 2) * LANES
        q_pair = q_ref[0, :, lo:lo + LANES]
        q_one = jnp.where(slot_of_lane == g % 2, q_pair, jnp.zeros_like(q_pair))
        if c < cached_chunks:
            kt = ktc_ref[0, lo:lo + LANES, keys(c)]
        else:
            kt = ktn_ref[lo:lo + LANES, keys(c - cached_chunks)]
        s = _dot(q_one, kt)
        s_ref[g % 2, :, keys(c)] = s
        for k in range(KEY_CHUNK // LANES):
            part = s[:, k * LANES:(k + 1) * LANES]
            run_max = part if run_max is None else jnp.maximum(run_max, part)
        return run_max

    def step(p_new, p_old):
        run_max = None
        for c in range(n_chunks):
            run_max = score_chunk(0, c, run_max)
        out = None
        den = None
        for g in range(N_KV_HEADS):
            row_max = jnp.max(run_max, axis=-1, keepdims=True)
            run_max = None
            acc = None
            for c in range(n_chunks):
                p_new[g, :, keys(c)] = jnp.exp2(s_ref[g % 2, :, keys(c)] - row_max).astype(BF16)
                if g + 1 < N_KV_HEADS:
                    run_max = score_chunk(g + 1, c, run_max)
                pv = _dot(p_old[g, :, keys(c)], vm_ref[g, keys(c), :])
                acc = pv if acc is None else acc + pv
            own = _group_of_lane() == g
            row_sum = acc[:, _sum_lane(g):_sum_lane(g) + 1]
            out = acc if out is None else jnp.where(own, acc, out)
            den = row_sum if den is None else jnp.where(own, row_sum, den)
        o_ref[0] = (out / den).astype(BF16)

    @pl.when(t % 2 == 0)
    def _():
        step(pa_ref, pb_ref)

    @pl.when(t % 2 == 1)
    def _():
        step(pb_ref, pa_ref)


def _attention_pipelined(q, ktc, ktn, vc, vn, tq):
    bsz, n, _ = q.shape
    t_cached = ktc.shape[-1]
    assert t_cached % KEY_CHUNK == 0 and n % KEY_CHUNK == 0
    t_keys = t_cached + n
    tiles = n // tq
    steps_per_batch = GROUP * tiles
    n_items = bsz * steps_per_batch

    def item(t):
        return t // steps_per_batch, (t // tiles) % GROUP, t % tiles

    def cur(t):
        return item(jnp.minimum(t, n_items - 1))

    def prev(t):
        return item(jnp.clip(t - 1, 0, n_items - 1))

    def q_map(t):
        b, j, i = cur(t)
        return b, i, j

    def o_map(t):
        b, j, i = prev(t)
        return b, i, j

    return pl.pallas_call(
        functools.partial(_attn_pipe_kernel, steps_per_batch),
        grid=(n_items + 1,),
        in_specs=[
            pl.BlockSpec((1, tq, MXU_N), q_map),
            pl.BlockSpec((1, KV_WIDTH, t_cached), lambda t: (cur(t)[0], 0, 0)),
            pl.BlockSpec((KV_WIDTH, n), lambda t: (0, cur(t)[0])),
            pl.BlockSpec((1, t_cached, KV_WIDTH), lambda t: (prev(t)[0], 0, 0)),
            pl.BlockSpec((n, KV_WIDTH), lambda t: (prev(t)[0], 0)),
        ],
        out_specs=pl.BlockSpec((1, tq, MXU_N), o_map),
        out_shape=jax.ShapeDtypeStruct(q.shape, BF16),
        scratch_shapes=[pltpu.VMEM((N_KV_HEADS, t_keys, MXU_N), BF16),
                        pltpu.VMEM((N_KV_HEADS, tq, t_keys), BF16), pltpu.VMEM((N_KV_HEADS, tq, t_keys), BF16),
                        pltpu.VMEM((2, tq, t_keys), F32)],
        compiler_params=_params(1),
        name=f"attn_pipe_t{t_keys}",
    )(q, ktc, ktn, vc, vn)


def _mix_out_kernel(att_ref, cin_ref, sa_ref, sg_ref, h_ref, g1_ref, sh2_ref, sc2_ref, gpost_ref, gpre_ref, wa_ref, wc_ref,
                    wo_ref, h1_ref, u2_ref):
    tm = h_ref.shape[0]
    halves = [slice(0, tm // 2), slice(tm // 2, tm)]
    cnv = [_dot(cin_ref[r, :], wc_ref[...]) for r in halves]
    att = [_dot(att_ref[r, :], wa_ref[...]) for r in halves]

    mixed = []
    half = D_MODEL // 2
    for r, a, c in zip(halves, att, cnv):
        merged = (sa_ref[r, :].astype(F32) * a + sg_ref[r, :].astype(F32) * c).astype(BF16)
        mixed.append(jnp.concatenate([_dot(merged, wo_ref[:, n * half:(n + 1) * half]) for n in range(2)], axis=1))

    for r, mo in zip(halves, mixed):
        h1 = h_ref[r, :] + g1_ref[0] * (_rms_rows(mo) * gpost_ref[...])
        h1_ref[r, :] = h1
        u2 = _rms_rows(h1) * (gpre_ref[...] * (1.0 + sc2_ref[0])) + sh2_ref[0]
        u2_ref[r, :] = u2.astype(BF16)


def _halo_specs(tm, n_rows, width):
    per = tm // HALO
    last = n_rows // HALO - 1
    prev = pl.BlockSpec((HALO, width), lambda i: (jnp.maximum(i * per - 1, 0), 0))
    nxt = pl.BlockSpec((HALO, width), lambda i: (jnp.minimum((i + 1) * per, last), 0))
    return prev, nxt


def _mix_out(att, conv_in, sa, sg, h, ada3, ada_row, tm, g_post1, g_pre2, w_att_out, w_conv_out, w_o):
    n = h.shape[0]
    wide = pl.BlockSpec((tm, D_MODEL), lambda i: (i, 0))
    ada = lambda k: pl.BlockSpec((1, 1, D_MODEL), lambda i: (ada_row(i), 0, k))
    sq = (D_MODEL, D_MODEL)
    return pl.pallas_call(
        _mix_out_kernel,
        grid=(n // tm,),
        in_specs=[wide, wide, wide, wide, wide, ada(2), ada(3), ada(4), _resident((1, D_MODEL)), _resident((1, D_MODEL)),
                  _resident(sq), _resident(sq), _resident(sq)],
        out_specs=[wide, wide],
        out_shape=[jax.ShapeDtypeStruct((n, D_MODEL), F32), jax.ShapeDtypeStruct((n, D_MODEL), BF16)],
        compiler_params=_params(1),
        name=f"mix_out_t{tm}_n{n}",
    )(att, conv_in, sa, sg, h, ada3, ada3, ada3, g_post1, g_pre2, w_att_out, w_conv_out, w_o)


FF_CHUNKS = (4 * MXU_N, 4 * MXU_N, 3 * MXU_N)
assert sum(FF_CHUNKS) == D_FF


def _ffn_kernel(tiles_per_seq, u_ref, up_ref, un_ref, h_ref, g2_ref, gpost_ref, cw_ref, wup_ref, wdn_ref, o_ref, ext_ref):
    i = pl.program_id(0)
    tm = u_ref.shape[0]
    has_prev = i % tiles_per_seq != 0
    has_next = i % tiles_per_seq != tiles_per_seq - 1
    zeros = jnp.zeros((HALO, D_MODEL), BF16)
    ext_ref[0:HALO, :] = jnp.where(has_prev, up_ref[...], zeros)
    ext_ref[HALO:HALO + tm, :] = u_ref[...]
    ext_ref[HALO + tm:, :] = jnp.where(has_next, un_ref[...], zeros)
    ext = ext_ref[...]
    rows = tm + 2 * HALO

    def up(lo, width):
        return [_dot(ext, wup_ref[:, col:col + width]) for col in (lo, D_FF + lo)]

    def conv(z, col, width):
        w = cw_ref[:, col:col + width]
        before = pltpu.roll(z, 1, axis=0)[HALO:HALO + tm]
        after = pltpu.roll(z, rows - 1, axis=0)[HALO:HALO + tm]
        return w[0:1] * before + w[1:2] * z[HALO:HALO + tm] + w[2:3] * after

    def down(lo, width, z_gate, z_val):
        gate = conv(z_gate, lo, width)
        val = conv(z_val, D_FF + lo, width)
        act = (gate * jax.nn.sigmoid(gate) * val).astype(BF16)
        half = D_MODEL // 2
        return [_dot(act, wdn_ref[lo:lo + width, n * half:(n + 1) * half]) for n in range(2)]

    starts = [sum(FF_CHUNKS[:c]) for c in range(len(FF_CHUNKS))]
    ff = None
    z_next = up(starts[0], FF_CHUNKS[0])
    for c, (lo, width) in enumerate(zip(starts, FF_CHUNKS)):
        z_cur = z_next
        if c + 1 < len(FF_CHUNKS):
            z_next = up(starts[c + 1], FF_CHUNKS[c + 1])
        parts = down(lo, width, *z_cur)
        ff = parts if ff is None else [a + b for a, b in zip(ff, parts)]
    ff = jnp.concatenate(ff, axis=1)

    o_ref[...] = h_ref[...] + g2_ref[0] * (_rms_rows(ff) * gpost_ref[...])


def _ffn(u2, h1, ada3, ada_row, seq_len, tm, g_post2, conv_ffn, w_up, w_down):
    n = h1.shape[0]
    assert seq_len % tm == 0
    row = lambda i: (i, 0)
    wide = pl.BlockSpec((tm, D_MODEL), row)
    prev, nxt = _halo_specs(tm, n, D_MODEL)
    return pl.pallas_call(
        functools.partial(_ffn_kernel, seq_len // tm),
        grid=(n // tm,),
        in_specs=[wide, prev, nxt, wide, pl.BlockSpec((1, 1, D_MODEL), lambda i: (ada_row(i), 0, 5)),
                  _resident((1, D_MODEL)), _resident((3, 2 * D_FF)),
                  _resident((D_MODEL, 2 * D_FF)), _resident((D_FF, D_MODEL))],
        out_specs=wide,
        out_shape=jax.ShapeDtypeStruct((n, D_MODEL), F32),
        scratch_shapes=[pltpu.VMEM((tm + 2 * HALO, D_MODEL), BF16)],
        compiler_params=_params(1),
        name=f"ffn_s{seq_len}",
    )(u2, u2, u2, h1, ada3, g_post2, conv_ffn, w_up, w_down)


def _rope_tables(n):
    pos = np.arange(n)
    inv = np.power(ROPE_THETA, -np.arange(0, AXIS_DIM, 2, dtype=np.float64) / AXIS_DIM)
    ang_r = (pos // GRID_W)[:, None] * inv[None, :]
    ang_c = (pos % GRID_W)[:, None] * inv[None, :]
    ang = np.concatenate([ang_r, ang_r, ang_c, ang_c], axis=1)
    sign = np.tile(np.concatenate([-np.ones(AXIS_DIM // 2), np.ones(AXIS_DIM // 2)]), 2)
    reps = LANES // HEAD_DIM
    cos = np.tile(np.cos(ang), (1, reps)).astype(np.float32)
    sin = np.tile(np.sin(ang) * sign[None, :], (1, reps)).astype(np.float32)
    return jnp.asarray(cos), jnp.asarray(sin)


def _segment_mean_matrix():
    head = np.arange(MXU_N) // HEAD_DIM
    return jnp.asarray((head[:, None] == head[None, :]).astype(np.float32) / HEAD_DIM, dtype=BF16)


def _layer(x, ada3, ada_row, seq_len, tiles, w, rope_tabs, cache_kv):
    bsz = x.shape[0]
    n = bsz * seq_len
    xf = x.reshape(n, D_MODEL)
    tm_in, tq, sets_per_step, tm_mix, tm_ffn = tiles
    q, kt, v, conv_in, sa, sg, *transposed = _in_proj(xf, ada3, lambda i: ada_row(i, tm_in), seq_len, tm_in, w["wq"],
                                                      w["w_in"], w["g_pre1"], w["qg"], w["kg"], w["seg"], w["conv_w"],
                                                      rope_tabs)
    q3 = q.reshape(bsz, seq_len, ATT_WIDTH)
    if cache_kv is None:
        att = _attention(q3, kt, v, tq, sets_per_step)
    else:
        ck, cv = cache_kv
        ktc = ck.reshape(bsz, -1, KV_WIDTH).transpose(0, 2, 1).astype(BF16)
        vc = cv.reshape(bsz, -1, KV_WIDTH).astype(BF16)
        att = _attention_pipelined(q3, ktc, kt, vc, v, tq)
    att = att.reshape(n, ATT_WIDTH)
    h1, u2 = _mix_out(att, conv_in, sa, sg, xf, ada3, lambda i: ada_row(i, tm_mix), tm_mix, w["g_post1"], w["g_pre2"],
                      w["w_att_out"], w["w_conv_out"], w["w_o"])
    out = _ffn(u2, h1, ada3, lambda i: ada_row(i, tm_ffn), seq_len, tm_ffn, w["g_post2"], w["conv_ffn"], w["w_up"],
               w["w_down"])
    return (out.reshape(x.shape), *transposed)


def kernel(x_prompt, x_sample, cache_k, cache_v, c, c_ctx, w_ada, b_ada, g_pre1, g_post1, g_pre2, g_post2, w_in, q_norm,
           k_norm, w_att_out, conv_w, w_conv_out, w_o, w_up, conv_ffn, w_down):
    depth = w_in.shape[0]
    dec_batch, dec_seq, _ = x_sample.shape
    batch, seq, _ = x_prompt.shape
    rope_tabs = _rope_tables(dec_seq)
    seg = _segment_mean_matrix()
    cc = jnp.zeros((ADA_ROWS, D_MODEL), F32).at[0].set(c_ctx).at[1:1 + dec_batch].set(c)
    reps = MXU_N // HEAD_DIM

    h_p, h_s = x_prompt, x_sample
    new_ks, new_vs = [], []
    for i in range(depth):
        wq = w_in[i][:, :ATT_WIDTH].reshape(D_MODEL, N_KV_HEADS, GROUP, HEAD_DIM).transpose(0, 2, 1, 3)
        wa = w_att_out[i].reshape(N_KV_HEADS, GROUP, HEAD_DIM, D_MODEL).transpose(1, 0, 2, 3)
        w = {
            "g_pre1": g_pre1[i][None], "g_post1": g_post1[i][None], "g_pre2": g_pre2[i][None], "g_post2": g_post2[i][None],
            "wq": wq.reshape(D_MODEL, ATT_WIDTH).astype(BF16), "w_in": w_in[i].astype(BF16),
            "qg": jnp.tile(q_norm[i] * (HEAD_DIM ** -0.5 * LOG2E), reps)[None],
            "kg": jnp.tile(k_norm[i], reps)[None],
            "seg": seg,
            "w_att_out": wa.reshape(ATT_WIDTH, D_MODEL).astype(BF16), "conv_w": conv_w[i],
            "w_conv_out": w_conv_out[i].astype(BF16),
            "w_o": w_o[i].astype(BF16), "w_up": w_up[i].astype(BF16), "conv_ffn": conv_ffn[i],
            "w_down": w_down[i].astype(BF16),
        }
        ada3 = _ada(cc, w_ada[i], b_ada[i]).reshape(ADA_ROWS, N_ADA, D_MODEL).reshape(ADA_ROWS, 1, N_ADA * D_MODEL)
        h_p, kt_ctx, vt_ctx = _layer(h_p, ada3, lambda t, tm: 0, seq, (512, seq, GROUP, 512, seq), w, None, None)
        new_ks.append(kt_ctx.reshape(batch, N_KV_HEADS, HEAD_DIM, seq).transpose(0, 3, 1, 2))
        new_vs.append(vt_ctx.reshape(batch, N_KV_HEADS, HEAD_DIM, seq).transpose(0, 3, 1, 2))
        (h_s,) = _layer(h_s, ada3, lambda t, tm: 1 + t // (dec_seq // tm), dec_seq, (512, 512, None, 512, 512), w,
                        rope_tabs, (cache_k[:, i], cache_v[:, i]))
    return (h_p, h_s, jnp.stack(new_ks, axis=1), jnp.stack(new_vs, axis=1))
```

```python
import functools
from typing import NamedTuple, Optional

import numpy as np
import jax
import jax.numpy as jnp
from jax.experimental import pallas as pl
from jax.experimental.pallas import tpu as pltpu

D_MODEL = 1024
N_HEADS = 16
N_KV_HEADS = 4
HEAD_DIM = 64
GROUP = N_HEADS // N_KV_HEADS
ATT_WIDTH = N_HEADS * HEAD_DIM
KV_WIDTH = N_KV_HEADS * HEAD_DIM
D_FF = 2816
GRID_W = 64
ROPE_THETA = 10000.0
AXIS_DIM = HEAD_DIM // 2
N_ADA = 6
EPS = 1e-6
LOG2E = 1.4426950408889634

OFF_Q = 0
OFF_K = OFF_Q + ATT_WIDTH
OFF_V = OFF_K + KV_WIDTH
OFF_B = OFF_V + KV_WIDTH
OFF_C = OFF_B + D_MODEL
OFF_X = OFF_C + D_MODEL
OFF_GA = OFF_X + D_MODEL
OFF_GC = OFF_GA + D_MODEL
IN_WIDTH = OFF_GC + D_MODEL

LANES = 128
MXU_N = 256
HALO = 16
ADA_ROWS = 16
VMEM_LIMIT = 56 * 1024 * 1024
KEY_CHUNK = 512
ROW_TILE = 1024
Q_TILE = 512


class _Tiles(NamedTuple):
    in_proj: int
    attn_q: int
    attn_sets: Optional[int]
    mix_out: int
    ffn: int

BF16 = jnp.bfloat16
F32 = jnp.float32


def _dot(a, b):
    return jnp.dot(a, b, preferred_element_type=F32)


def _resident(shape):
    nd = len(shape)
    return pl.BlockSpec(shape, lambda *_: (0,) * nd, pipeline_mode=pl.Buffered(1))


def _params(n_axes):
    return pltpu.CompilerParams(dimension_semantics=("arbitrary",) * n_axes, vmem_limit_bytes=VMEM_LIMIT)


def _ada_kernel(c_ref, w_ref, b_ref, o_ref):
    c = c_ref[...]
    s = (c * jax.nn.sigmoid(c)).astype(BF16)
    o_ref[...] = _dot(s, w_ref[...].astype(BF16)) + b_ref[...]


def _ada(cc, w_ada, b_ada):
    n = w_ada.shape[1]
    tn = D_MODEL
    return pl.pallas_call(
        _ada_kernel,
        grid=(n // tn,),
        in_specs=[
            pl.BlockSpec((ADA_ROWS, D_MODEL), lambda j: (0, 0)),
            pl.BlockSpec((D_MODEL, tn), lambda j: (0, j)),
            pl.BlockSpec((1, tn), lambda j: (0, j)),
        ],
        out_specs=pl.BlockSpec((ADA_ROWS, tn), lambda j: (0, j)),
        out_shape=jax.ShapeDtypeStruct((ADA_ROWS, n), F32),
        compiler_params=_params(1),
        name="ada",
    )(cc, w_ada, b_ada.reshape(1, n))


def _rms_rows(x):
    return x * jax.lax.rsqrt(jnp.mean(x * x, axis=-1, keepdims=True) + EPS)


def _head_norm(z, seg_ref, gain):
    ms = _dot((z * z).astype(BF16), seg_ref[...])
    return z * jax.lax.rsqrt(ms + EPS) * gain


def _rope(z, cos, sin, first_half):
    partner = jnp.where(first_half, pltpu.roll(z, LANES - AXIS_DIM // 2, axis=1), pltpu.roll(z, AXIS_DIM // 2, axis=1))
    return z * cos + partner * sin


def _in_proj_kernel(rope, tiles_per_seq, period, x_ref, xp_ref, xn_ref, sh_ref, sc_ref, gpre_ref, wq_ref, w_ref, qg_ref,
                    kg_ref, seg_ref, cw_ref, *rest):
    if rope:
        cos_ref, sin_ref, q_ref, kt_ref, v_ref, cin_ref, sa_ref, sg_ref, ext_ref = rest
        cos = cos_ref[...]
        sin = sin_ref[...]
        lane = jax.lax.broadcasted_iota(jnp.int32, (1, LANES), 1)
        first_half = (lane % AXIS_DIM) < (AXIS_DIM // 2)
    else:
        q_ref, kt_ref, v_ref, cin_ref, sa_ref, sg_ref, ktf_ref, vtf_ref, ext_ref = rest

    i = pl.program_id(0)
    tm = x_ref.shape[0]
    rows = tm + 2 * HALO
    mod = gpre_ref[...] * (1.0 + sc_ref[0])

    def pre_norm(x):
        return (_rms_rows(x) * mod + sh_ref[0]).astype(BF16)

    has_prev = i % tiles_per_seq != 0
    has_next = i % tiles_per_seq != tiles_per_seq - 1
    zeros = jnp.zeros((HALO, D_MODEL), BF16)
    ext_ref[0:HALO, :] = jnp.where(has_prev, pre_norm(xp_ref[...]), zeros)
    ext_ref[HALO:HALO + tm, :] = pre_norm(x_ref[...])
    ext_ref[HALO + tm:, :] = jnp.where(has_next, pre_norm(xn_ref[...]), zeros)
    u = ext_ref[HALO:HALO + tm, :]
    u_ext = ext_ref[...]

    def rotate(z):
        if not rope:
            return z
        return jnp.concatenate(
            [_rope(z[:, h * LANES:(h + 1) * LANES], cos, sin, first_half) for h in range(MXU_N // LANES)], axis=1)

    def cols(j):
        return slice(j * MXU_N, (j + 1) * MXU_N)

    def store_per_sequence(ref, zt):
        seq = ref.shape[-1]
        for s in range(ref.shape[0]):
            ref[s] = zt[:, s * seq:(s + 1) * seq]

    def store_q(j, z):
        q_ref[:, cols(j)] = rotate(_head_norm(z, seg_ref, qg_ref[...])).astype(BF16)

    def store_k(z):
        k = _head_norm(z, seg_ref, kg_ref[...])
        kt_ref[...] = rotate(k).T.astype(BF16)
        if not rope:
            store_per_sequence(ktf_ref, k.T)

    def store_v(z):
        v_ref[...] = z.astype(BF16)
        if not rope:
            store_per_sequence(vtf_ref, z.T)

    def store_conv_in(j, zb, zc, zx):
        w = cw_ref[:, cols(j)]
        y = zc * zx
        before = pltpu.roll(y, 1, axis=0)[HALO:HALO + tm]
        after = pltpu.roll(y, rows - 1, axis=0)[HALO:HALO + tm]
        if period < tm:
            t = jax.lax.broadcasted_iota(jnp.int32, (tm, 1), 0) % period
            before = jnp.where(t == 0, 0.0, before)
            after = jnp.where(t == period - 1, 0.0, after)
        conv = w[0:1] * before + w[1:2] * y[HALO:HALO + tm] + w[2:3] * after
        cin_ref[:, cols(j)] = (zb * conv).astype(BF16)

    def store_gate(ref, j, z):
        ref[:, cols(j)] = jax.nn.sigmoid(z).astype(BF16)

    work = [([(u, wq_ref, j * MXU_N)], functools.partial(store_q, j)) for j in range(ATT_WIDTH // MXU_N)]
    work += [([(u, w_ref, OFF_K)], store_k)]
    for j in range(D_MODEL // MXU_N):
        conv_operands = [(u, w_ref, OFF_B + j * MXU_N), (u_ext, w_ref, OFF_C + j * MXU_N), (u_ext, w_ref, OFF_X + j * MXU_N)]
        work += [(conv_operands, functools.partial(store_conv_in, j)),
                 ([(u, w_ref, OFF_GA + j * MXU_N)], functools.partial(store_gate, sa_ref, j)),
                 ([(u, w_ref, OFF_GC + j * MXU_N)], functools.partial(store_gate, sg_ref, j))]
    work += [([(u, w_ref, OFF_V)], store_v)]
    pending = None
    for operands, consume in work:
        products = [_dot(lhs, weights[:, lo:lo + MXU_N]) for lhs, weights, lo in operands]
        if pending is not None:
            pending[0](*pending[1])
        pending = (consume, products)
    pending[0](*pending[1])


def _in_proj(x, ada3, ada_row, seq_len, tm, wq, w_in, g_pre1, qg, kg, seg, conv_w, rope_tabs):
    n = x.shape[0]
    rope = rope_tabs is not None
    assert seq_len % tm == 0 or tm % seq_len == 0
    tiles_per_seq = max(seq_len // tm, 1)
    row = lambda i: (i, 0)
    prev, nxt = _halo_specs(tm, n, D_MODEL)
    in_specs = [
        pl.BlockSpec((tm, D_MODEL), row), prev, nxt,
        pl.BlockSpec((1, 1, D_MODEL), lambda i: (ada_row(i), 0, 0)),
        pl.BlockSpec((1, 1, D_MODEL), lambda i: (ada_row(i), 0, 1)),
        _resident((1, D_MODEL)),
        _resident((D_MODEL, ATT_WIDTH)),
        _resident((D_MODEL, IN_WIDTH)),
        _resident((1, MXU_N)),
        _resident((1, MXU_N)),
        _resident((MXU_N, MXU_N)),
        _resident((3, D_MODEL)),
    ]
    args = [x, x, x, ada3, ada3, g_pre1, wq, w_in, qg, kg, seg, conv_w]
    wide = jax.ShapeDtypeStruct((n, D_MODEL), BF16)
    wide_spec = pl.BlockSpec((tm, D_MODEL), row)
    out_shape = [wide, jax.ShapeDtypeStruct((KV_WIDTH, n), BF16), jax.ShapeDtypeStruct((n, KV_WIDTH), BF16)] + [wide] * 3
    out_specs = [wide_spec, pl.BlockSpec((KV_WIDTH, tm), lambda i: (0, i)), pl.BlockSpec((tm, KV_WIDTH), row)]
    out_specs += [wide_spec] * 3
    if rope:
        assert seq_len % tm == 0
        in_specs += [pl.BlockSpec((tm, LANES), lambda i: (i % tiles_per_seq, 0))] * 2
        args += list(rope_tabs)
    else:
        assert tm % seq_len == 0
        per_tile = tm // seq_len
        transposed = jax.ShapeDtypeStruct((n // seq_len, KV_WIDTH, seq_len), F32)
        out_shape += [transposed] * 2
        out_specs += [pl.BlockSpec((per_tile, KV_WIDTH, seq_len), lambda i: (i, 0, 0))] * 2
    return pl.pallas_call(
        functools.partial(_in_proj_kernel, rope, tiles_per_seq, min(seq_len, tm)),
        grid=(n // tm,),
        in_specs=in_specs,
        out_specs=out_specs,
        out_shape=out_shape,
        scratch_shapes=[pltpu.VMEM((tm + 2 * HALO, D_MODEL), BF16)],
        compiler_params=_params(1),
        name="in_proj_rope" if rope else "in_proj",
    )(*args)


def _scores(q_ref, kt_ref, col0, g):
    lo = (g // 2) * LANES
    q_pair = q_ref[0, :, col0 + lo:col0 + lo + LANES]
    slot = jax.lax.broadcasted_iota(jnp.int32, (1, LANES), 1) // HEAD_DIM
    q_one = jnp.where(slot == g % 2, q_pair, jnp.zeros_like(q_pair))
    return _dot(q_one, kt_ref[lo:lo + LANES, :])


def _group_of_lane():
    return jax.lax.broadcasted_iota(jnp.int32, (1, MXU_N), 1) // HEAD_DIM


def _sum_lane(g):
    return (HEAD_DIM * (g + 1)) % MXU_N


def _build_masked_v(pieces, vm_ref):
    lane = jax.lax.broadcasted_iota(jnp.int32, (1, MXU_N), 1)
    for row0, v in pieces:
        v = v.astype(F32)
        for g in range(N_KV_HEADS):
            ones_col = (lane == _sum_lane(g)).astype(F32)
            vm_ref[g, row0:row0 + v.shape[0], :] = jnp.where(_group_of_lane() == g, v, ones_col).astype(BF16)


def _unnormalised_probs(s):
    return jnp.exp2(s - jnp.max(s, axis=-1, keepdims=True)).astype(BF16)


def _weighted_values(probs, vm_ref):
    acc = None
    den = None
    for g in range(N_KV_HEADS):
        pv = _dot(probs[g], vm_ref[g])
        own = _group_of_lane() == g
        row_sum = pv[:, _sum_lane(g):_sum_lane(g) + 1]
        acc = pv if acc is None else jnp.where(own, pv, acc)
        den = row_sum if den is None else jnp.where(own, row_sum, den)
    return acc / den


def _attn_kernel(n_sets, q_ref, kt_ref, v_ref, o_ref, vm_ref):
    @pl.when((pl.program_id(1) == 0) & (pl.program_id(2) == 0))
    def _():
        _build_masked_v([(0, v_ref[...])], vm_ref)

    def probs_of(j):
        return [_unnormalised_probs(_scores(q_ref, kt_ref, j * MXU_N, g)) for g in range(N_KV_HEADS)]

    probs_next = probs_of(0)
    for j in range(n_sets):
        probs = probs_next
        if j + 1 < n_sets:
            probs_next = probs_of(j + 1)
        o_ref[0, :, j * MXU_N:(j + 1) * MXU_N] = _weighted_values(probs, vm_ref).astype(BF16)


def _attention(q, kt, v, tq, sets_per_step):
    bsz, n, _ = q.shape
    t = kt.shape[-1] // bsz
    width = sets_per_step * MXU_N
    return pl.pallas_call(
        functools.partial(_attn_kernel, sets_per_step),
        grid=(bsz, GROUP // sets_per_step, n // tq),
        in_specs=[
            pl.BlockSpec((1, tq, width), lambda b, j, i: (b, i, j)),
            pl.BlockSpec((KV_WIDTH, t), lambda b, j, i: (0, b)),
            pl.BlockSpec((t, KV_WIDTH), lambda b, j, i: (b, 0)),
        ],
        out_specs=pl.BlockSpec((1, tq, width), lambda b, j, i: (b, i, j)),
        out_shape=jax.ShapeDtypeStruct(q.shape, BF16),
        scratch_shapes=[pltpu.VMEM((N_KV_HEADS, t, MXU_N), BF16)],
        compiler_params=_params(3),
        name=f"attn_t{t}",
    )(q, kt, v)


def _attn_pipe_kernel(steps_per_batch, q_ref, ktc_ref, ktn_ref, vc_ref, vn_ref, o_ref, vm_ref, pa_ref, pb_ref, s_ref):
    t = pl.program_id(0)

    @pl.when(t == 0)
    def _():
        pb_ref[...] = jnp.ones_like(pb_ref)

    @pl.when((t == 0) | ((t - 1) % steps_per_batch == 0))
    def _():
        _build_masked_v([(0, vc_ref[0]), (vc_ref.shape[1], vn_ref[...])], vm_ref)

    cached_chunks = ktc_ref.shape[-1] // KEY_CHUNK
    n_chunks = cached_chunks + ktn_ref.shape[-1] // KEY_CHUNK
    slot_of_lane = jax.lax.broadcasted_iota(jnp.int32, (1, LANES), 1) // HEAD_DIM

    def keys(c):
        return slice(c * KEY_CHUNK, (c + 1) * KEY_CHUNK)

    def score_chunk(g, c, run_max):
        lo = (g // 2) * LANES
        q_pair = q_ref[0, :, lo:lo + LANES]
        q_one = jnp.where(slot_of_lane == g % 2, q_pair, jnp.zeros_like(q_pair))
        if c < cached_chunks:
            kt = ktc_ref[0, lo:lo + LANES, keys(c)]
        else:
            kt = ktn_ref[lo:lo + LANES, keys(c - cached_chunks)]
        s = _dot(q_one, kt)
        s_ref[g % 2, :, keys(c)] = s
        for k in range(KEY_CHUNK // LANES):
            part = s[:, k * LANES:(k + 1) * LANES]
            run_max = part if run_max is None else jnp.maximum(run_max, part)
        return run_max

    def step(p_new, p_old):
        run_max = None
        for c in range(n_chunks):
            run_max = score_chunk(0, c, run_max)
        out = None
        den = None
        for g in range(N_KV_HEADS):
            row_max = jnp.max(run_max, axis=-1, keepdims=True)
            run_max = None
            acc = None
            for c in range(n_chunks):
                p_new[g, :, keys(c)] = jnp.exp2(s_ref[g % 2, :, keys(c)] - row_max).astype(BF16)
                if g + 1 < N_KV_HEADS:
                    run_max = score_chunk(g + 1, c, run_max)
                pv = _dot(p_old[g, :, keys(c)], vm_ref[g, keys(c), :])
                acc = pv if acc is None else acc + pv
            own = _group_of_lane() == g
            row_sum = acc[:, _sum_lane(g):_sum_lane(g) + 1]
            out = acc if out is None else jnp.where(own, acc, out)
            den = row_sum if den is None else jnp.where(own, row_sum, den)
        o_ref[0] = (out / den).astype(BF16)

    @pl.when(t % 2 == 0)
    def _():
        step(pa_ref, pb_ref)

    @pl.when(t % 2 == 1)
    def _():
        step(pb_ref, pa_ref)


def _attention_pipelined(q, ktc, ktn, vc, vn, tq):
    bsz, n, _ = q.shape
    t_cached = ktc.shape[-1]
    assert t_cached % KEY_CHUNK == 0 and n % KEY_CHUNK == 0
    t_keys = t_cached + n
    tiles = n // tq
    steps_per_batch = GROUP * tiles
    n_items = bsz * steps_per_batch

    def item(t):
        return t // steps_per_batch, (t // tiles) % GROUP, t % tiles

    def cur(t):
        return item(jnp.minimum(t, n_items - 1))

    def prev(t):
        return item(jnp.clip(t - 1, 0, n_items - 1))

    def q_map(t):
        b, j, i = cur(t)
        return b, i, j

    def o_map(t):
        b, j, i = prev(t)
        return b, i, j

    return pl.pallas_call(
        functools.partial(_attn_pipe_kernel, steps_per_batch),
        grid=(n_items + 1,),
        in_specs=[
            pl.BlockSpec((1, tq, MXU_N), q_map),
            pl.BlockSpec((1, KV_WIDTH, t_cached), lambda t: (cur(t)[0], 0, 0)),
            pl.BlockSpec((KV_WIDTH, n), lambda t: (0, cur(t)[0])),
            pl.BlockSpec((1, t_cached, KV_WIDTH), lambda t: (prev(t)[0], 0, 0)),
            pl.BlockSpec((n, KV_WIDTH), lambda t: (prev(t)[0], 0)),
        ],
        out_specs=pl.BlockSpec((1, tq, MXU_N), o_map),
        out_shape=jax.ShapeDtypeStruct(q.shape, BF16),
        scratch_shapes=[pltpu.VMEM((N_KV_HEADS, t_keys, MXU_N), BF16),
                        pltpu.VMEM((N_KV_HEADS, tq, t_keys), BF16), pltpu.VMEM((N_KV_HEADS, tq, t_keys), BF16),
                        pltpu.VMEM((2, tq, t_keys), F32)],
        compiler_params=_params(1),
        name=f"attn_pipe_t{t_keys}",
    )(q, ktc, ktn, vc, vn)


def _mix_out_kernel(att_ref, cin_ref, sa_ref, sg_ref, h_ref, g1_ref, sh2_ref, sc2_ref, gpost_ref, gpre_ref, wa_ref, wc_ref,
                    wo_ref, h1_ref, u2_ref):
    tm = h_ref.shape[0]
    halves = [slice(0, tm // 2), slice(tm // 2, tm)]
    cnv = [_dot(cin_ref[r, :], wc_ref[...]) for r in halves]
    att = [_dot(att_ref[r, :], wa_ref[...]) for r in halves]

    mixed = []
    half = D_MODEL // 2
    for r, a, c in zip(halves, att, cnv):
        merged = (sa_ref[r, :].astype(F32) * a + sg_ref[r, :].astype(F32) * c).astype(BF16)
        mixed.append(jnp.concatenate([_dot(merged, wo_ref[:, n * half:(n + 1) * half]) for n in range(2)], axis=1))

    for r, mo in zip(halves, mixed):
        h1 = h_ref[r, :] + g1_ref[0] * (_rms_rows(mo) * gpost_ref[...])
        h1_ref[r, :] = h1
        u2 = _rms_rows(h1) * (gpre_ref[...] * (1.0 + sc2_ref[0])) + sh2_ref[0]
        u2_ref[r, :] = u2.astype(BF16)


def _halo_specs(tm, n_rows, width):
    per = tm // HALO
    last = n_rows // HALO - 1
    prev = pl.BlockSpec((HALO, width), lambda i: (jnp.maximum(i * per - 1, 0), 0))
    nxt = pl.BlockSpec((HALO, width), lambda i: (jnp.minimum((i + 1) * per, last), 0))
    return prev, nxt


def _mix_out(att, conv_in, sa, sg, h, ada3, ada_row, tm, g_post1, g_pre2, w_att_out, w_conv_out, w_o):
    n = h.shape[0]
    wide = pl.BlockSpec((tm, D_MODEL), lambda i: (i, 0))
    ada = lambda k: pl.BlockSpec((1, 1, D_MODEL), lambda i: (ada_row(i), 0, k))
    sq = (D_MODEL, D_MODEL)
    return pl.pallas_call(
        _mix_out_kernel,
        grid=(n // tm,),
        in_specs=[wide, wide, wide, wide, wide, ada(2), ada(3), ada(4), _resident((1, D_MODEL)), _resident((1, D_MODEL)),
                  _resident(sq), _resident(sq), _resident(sq)],
        out_specs=[wide, wide],
        out_shape=[jax.ShapeDtypeStruct((n, D_MODEL), F32), jax.ShapeDtypeStruct((n, D_MODEL), BF16)],
        compiler_params=_params(1),
        name=f"mix_out_t{tm}_n{n}",
    )(att, conv_in, sa, sg, h, ada3, ada3, ada3, g_post1, g_pre2, w_att_out, w_conv_out, w_o)


FF_CHUNKS = (4 * MXU_N, 4 * MXU_N, 3 * MXU_N)
assert sum(FF_CHUNKS) == D_FF


def _ffn_kernel(tiles_per_seq, u_ref, up_ref, un_ref, h_ref, g2_ref, gpost_ref, cw_ref, wup_ref, wdn_ref, o_ref, ext_ref):
    i = pl.program_id(0)
    tm = u_ref.shape[0]
    has_prev = i % tiles_per_seq != 0
    has_next = i % tiles_per_seq != tiles_per_seq - 1
    zeros = jnp.zeros((HALO, D_MODEL), BF16)
    ext_ref[0:HALO, :] = jnp.where(has_prev, up_ref[...], zeros)
    ext_ref[HALO:HALO + tm, :] = u_ref[...]
    ext_ref[HALO + tm:, :] = jnp.where(has_next, un_ref[...], zeros)
    ext = ext_ref[...]
    rows = tm + 2 * HALO

    def up(lo, width):
        return [_dot(ext, wup_ref[:, col:col + width]) for col in (lo, D_FF + lo)]

    def conv(z, col, width):
        w = cw_ref[:, col:col + width]
        before = pltpu.roll(z, 1, axis=0)[HALO:HALO + tm]
        after = pltpu.roll(z, rows - 1, axis=0)[HALO:HALO + tm]
        return w[0:1] * before + w[1:2] * z[HALO:HALO + tm] + w[2:3] * after

    def down(lo, width, z_gate, z_val):
        gate = conv(z_gate, lo, width)
        val = conv(z_val, D_FF + lo, width)
        act = (gate * jax.nn.sigmoid(gate) * val).astype(BF16)
        half = D_MODEL // 2
        return [_dot(act, wdn_ref[lo:lo + width, n * half:(n + 1) * half]) for n in range(2)]

    starts = [sum(FF_CHUNKS[:c]) for c in range(len(FF_CHUNKS))]
    ff = None
    z_next = up(starts[0], FF_CHUNKS[0])
    for c, (lo, width) in enumerate(zip(starts, FF_CHUNKS)):
        z_cur = z_next
        if c + 1 < len(FF_CHUNKS):
            z_next = up(starts[c + 1], FF_CHUNKS[c + 1])
        parts = down(lo, width, *z_cur)
        ff = parts if ff is None else [a + b for a, b in zip(ff, parts)]
    ff = jnp.concatenate(ff, axis=1)

    o_ref[...] = h_ref[...] + g2_ref[0] * (_rms_rows(ff) * gpost_ref[...])


def _ffn(u2, h1, ada3, ada_row, seq_len, tm, g_post2, conv_ffn, w_up, w_down):
    n = h1.shape[0]
    assert seq_len % tm == 0
    row = lambda i: (i, 0)
    wide = pl.BlockSpec((tm, D_MODEL), row)
    prev, nxt = _halo_specs(tm, n, D_MODEL)
    return pl.pallas_call(
        functools.partial(_ffn_kernel, seq_len // tm),
        grid=(n // tm,),
        in_specs=[wide, prev, nxt, wide, pl.BlockSpec((1, 1, D_MODEL), lambda i: (ada_row(i), 0, 5)),
                  _resident((1, D_MODEL)), _resident((3, 2 * D_FF)),
                  _resident((D_MODEL, 2 * D_FF)), _resident((D_FF, D_MODEL))],
        out_specs=wide,
        out_shape=jax.ShapeDtypeStruct((n, D_MODEL), F32),
        scratch_shapes=[pltpu.VMEM((tm + 2 * HALO, D_MODEL), BF16)],
        compiler_params=_params(1),
        name=f"ffn_s{seq_len}",
    )(u2, u2, u2, h1, ada3, g_post2, conv_ffn, w_up, w_down)


def _rope_tables(n):
    pos = np.arange(n)
    inv = np.power(ROPE_THETA, -np.arange(0, AXIS_DIM, 2, dtype=np.float64) / AXIS_DIM)
    ang_r = (pos // GRID_W)[:, None] * inv[None, :]
    ang_c = (pos % GRID_W)[:, None] * inv[None, :]
    ang = np.concatenate([ang_r, ang_r, ang_c, ang_c], axis=1)
    sign = np.tile(np.concatenate([-np.ones(AXIS_DIM // 2), np.ones(AXIS_DIM // 2)]), 2)
    reps = LANES // HEAD_DIM
    cos = np.tile(np.cos(ang), (1, reps)).astype(np.float32)
    sin = np.tile(np.sin(ang) * sign[None, :], (1, reps)).astype(np.float32)
    return jnp.asarray(cos), jnp.asarray(sin)


def _segment_mean_matrix():
    head = np.arange(MXU_N) // HEAD_DIM
    return jnp.asarray((head[:, None] == head[None, :]).astype(np.float32) / HEAD_DIM, dtype=BF16)


def _layer(x, ada3, ada_row, seq_len, tiles, w, rope_tabs, cache_kv):
    bsz = x.shape[0]
    n = bsz * seq_len
    xf = x.reshape(n, D_MODEL)
    tm_in, tq, sets_per_step, tm_mix, tm_ffn = tiles
    q, kt, v, conv_in, sa, sg, *transposed = _in_proj(xf, ada3, lambda i: ada_row(i, tm_in), seq_len, tm_in, w["wq"],
                                                      w["w_in"], w["g_pre1"], w["qg"], w["kg"], w["seg"], w["conv_w"],
                                                      rope_tabs)
    q3 = q.reshape(bsz, seq_len, ATT_WIDTH)
    if cache_kv is None:
        att = _attention(q3, kt, v, tq, sets_per_step)
    else:
        ck, cv = cache_kv
        ktc = ck.reshape(bsz, -1, KV_WIDTH).transpose(0, 2, 1).astype(BF16)
        vc = cv.reshape(bsz, -1, KV_WIDTH).astype(BF16)
        att = _attention_pipelined(q3, ktc, kt, vc, v, tq)
    att = att.reshape(n, ATT_WIDTH)
    h1, u2 = _mix_out(att, conv_in, sa, sg, xf, ada3, lambda i: ada_row(i, tm_mix), tm_mix, w["g_post1"], w["g_pre2"],
                      w["w_att_out"], w["w_conv_out"], w["w_o"])
    out = _ffn(u2, h1, ada3, lambda i: ada_row(i, tm_ffn), seq_len, tm_ffn, w["g_post2"], w["conv_ffn"], w["w_up"],
               w["w_down"])
    return (out.reshape(x.shape), *transposed)


def kernel(x_prompt, x_sample, cache_k, cache_v, c, c_ctx, w_ada, b_ada, g_pre1, g_post1, g_pre2, g_post2, w_in, q_norm,
           k_norm, w_att_out, conv_w, w_conv_out, w_o, w_up, conv_ffn, w_down):
    depth = w_in.shape[0]
    dec_batch, dec_seq, _ = x_sample.shape
    batch, seq, _ = x_prompt.shape
    rope_tabs = _rope_tables(dec_seq)
    seg = _segment_mean_matrix()
    cc = jnp.zeros((ADA_ROWS, D_MODEL), F32).at[0].set(c_ctx).at[1:1 + dec_batch].set(c)
    reps = MXU_N // HEAD_DIM

    h_p, h_s = x_prompt, x_sample
    new_ks, new_vs = [], []
    for i in range(depth):
        wq = w_in[i][:, :ATT_WIDTH].reshape(D_MODEL, N_KV_HEADS, GROUP, HEAD_DIM).transpose(0, 2, 1, 3)
        wa = w_att_out[i].reshape(N_KV_HEADS, GROUP, HEAD_DIM, D_MODEL).transpose(1, 0, 2, 3)
        w = {
            "g_pre1": g_pre1[i][None], "g_post1": g_post1[i][None], "g_pre2": g_pre2[i][None], "g_post2": g_post2[i][None],
            "wq": wq.reshape(D_MODEL, ATT_WIDTH).astype(BF16), "w_in": w_in[i].astype(BF16),
            "qg": jnp.tile(q_norm[i] * (HEAD_DIM ** -0.5 * LOG2E), reps)[None],
            "kg": jnp.tile(k_norm[i], reps)[None],
            "seg": seg,
            "w_att_out": wa.reshape(ATT_WIDTH, D_MODEL).astype(BF16), "conv_w": conv_w[i],
            "w_conv_out": w_conv_out[i].astype(BF16),
            "w_o": w_o[i].astype(BF16), "w_up": w_up[i].astype(BF16), "conv_ffn": conv_ffn[i],
            "w_down": w_down[i].astype(BF16),
        }
        ada3 = _ada(cc, w_ada[i], b_ada[i]).reshape(ADA_ROWS, N_ADA, D_MODEL).reshape(ADA_ROWS, 1, N_ADA * D_MODEL)
        context_tiles = _Tiles(ROW_TILE, seq, GROUP, ROW_TILE, seq)
        h_p, kt_ctx, vt_ctx = _layer(h_p, ada3, lambda t, tm: 0, seq, context_tiles, w, None, None)
        new_ks.append(kt_ctx.reshape(batch, N_KV_HEADS, HEAD_DIM, seq).transpose(0, 3, 1, 2))
        new_vs.append(vt_ctx.reshape(batch, N_KV_HEADS, HEAD_DIM, seq).transpose(0, 3, 1, 2))
        latent_tiles = _Tiles(ROW_TILE, Q_TILE, None, ROW_TILE, ROW_TILE)
        (h_s,) = _layer(h_s, ada3, lambda t, tm: 1 + t // (dec_seq // tm), dec_seq, latent_tiles, w,
                        rope_tabs, (cache_k[:, i], cache_v[:, i]))
    return (h_p, h_s, jnp.stack(new_ks, axis=1), jnp.stack(new_vs, axis=1))
```

```python
import functools
from typing import NamedTuple, Optional

import numpy as np
import jax
import jax.numpy as jnp
from jax.experimental import pallas as pl
from jax.experimental.pallas import tpu as pltpu

D_MODEL = 1024
N_HEADS = 16
N_KV_HEADS = 4
HEAD_DIM = 64
GROUP = N_HEADS // N_KV_HEADS
ATT_WIDTH = N_HEADS * HEAD_DIM
KV_WIDTH = N_KV_HEADS * HEAD_DIM
D_FF = 2816
GRID_W = 64
ROPE_THETA = 10000.0
AXIS_DIM = HEAD_DIM // 2
N_ADA = 6
EPS = 1e-6
LOG2E = 1.4426950408889634

OFF_Q = 0
OFF_K = OFF_Q + ATT_WIDTH
OFF_V = OFF_K + KV_WIDTH
OFF_B = OFF_V + KV_WIDTH
OFF_C = OFF_B + D_MODEL
OFF_X = OFF_C + D_MODEL
OFF_GA = OFF_X + D_MODEL
OFF_GC = OFF_GA + D_MODEL
IN_WIDTH = OFF_GC + D_MODEL

LANES = 128
MXU_N = 256
HALO = 16
ADA_ROWS = 16
VMEM_LIMIT = 56 * 1024 * 1024
KEY_CHUNK = 256
ROW_TILE = 1024
Q_TILE = 512


class _Tiles(NamedTuple):
    in_proj: int
    attn_q: int
    attn_sets: Optional[int]
    mix_out: int
    ffn: int

BF16 = jnp.bfloat16
F32 = jnp.float32


def _dot(a, b):
    return jnp.dot(a, b, preferred_element_type=F32)


def _resident(shape):
    nd = len(shape)
    return pl.BlockSpec(shape, lambda *_: (0,) * nd, pipeline_mode=pl.Buffered(1))


def _params(n_axes):
    return pltpu.CompilerParams(dimension_semantics=("arbitrary",) * n_axes, vmem_limit_bytes=VMEM_LIMIT)


def _ada_kernel(c_ref, w_ref, b_ref, o_ref):
    c = c_ref[...]
    s = (c * jax.nn.sigmoid(c)).astype(BF16)
    o_ref[...] = _dot(s, w_ref[...].astype(BF16)) + b_ref[...]


def _ada(cc, w_ada, b_ada):
    n = w_ada.shape[1]
    tn = D_MODEL
    return pl.pallas_call(
        _ada_kernel,
        grid=(n // tn,),
        in_specs=[
            pl.BlockSpec((ADA_ROWS, D_MODEL), lambda j: (0, 0)),
            pl.BlockSpec((D_MODEL, tn), lambda j: (0, j)),
            pl.BlockSpec((1, tn), lambda j: (0, j)),
        ],
        out_specs=pl.BlockSpec((ADA_ROWS, tn), lambda j: (0, j)),
        out_shape=jax.ShapeDtypeStruct((ADA_ROWS, n), F32),
        compiler_params=_params(1),
        name="ada",
    )(cc, w_ada, b_ada.reshape(1, n))


def _rms_rows(x):
    return x * jax.lax.rsqrt(jnp.mean(x * x, axis=-1, keepdims=True) + EPS)


def _head_norm(z, seg_ref, gain):
    ms = _dot((z * z).astype(BF16), seg_ref[...])
    return z * jax.lax.rsqrt(ms + EPS) * gain


def _rope(z, cos, sin, first_half):
    partner = jnp.where(first_half, pltpu.roll(z, LANES - AXIS_DIM // 2, axis=1), pltpu.roll(z, AXIS_DIM // 2, axis=1))
    return z * cos + partner * sin


def _in_proj_kernel(rope, tiles_per_seq, period, x_ref, xp_ref, xn_ref, sh_ref, sc_ref, gpre_ref, wq_ref, w_ref, qg_ref,
                    kg_ref, seg_ref, cw_ref, *rest):
    if rope:
        cos_ref, sin_ref, q_ref, kt_ref, v_ref, cin_ref, sa_ref, sg_ref, ext_ref = rest
        cos = cos_ref[...]
        sin = sin_ref[...]
        lane = jax.lax.broadcasted_iota(jnp.int32, (1, LANES), 1)
        first_half = (lane % AXIS_DIM) < (AXIS_DIM // 2)
    else:
        q_ref, kt_ref, v_ref, cin_ref, sa_ref, sg_ref, ktf_ref, vtf_ref, ext_ref = rest

    i = pl.program_id(0)
    tm = x_ref.shape[0]
    rows = tm + 2 * HALO
    mod = gpre_ref[...] * (1.0 + sc_ref[0])

    def pre_norm(x):
        return (_rms_rows(x) * mod + sh_ref[0]).astype(BF16)

    has_prev = i % tiles_per_seq != 0
    has_next = i % tiles_per_seq != tiles_per_seq - 1
    zeros = jnp.zeros((HALO, D_MODEL), BF16)
    ext_ref[0:HALO, :] = jnp.where(has_prev, pre_norm(xp_ref[...]), zeros)
    ext_ref[HALO:HALO + tm, :] = pre_norm(x_ref[...])
    ext_ref[HALO + tm:, :] = jnp.where(has_next, pre_norm(xn_ref[...]), zeros)
    u = ext_ref[HALO:HALO + tm, :]
    u_ext = ext_ref[...]

    def rotate(z):
        if not rope:
            return z
        return jnp.concatenate(
            [_rope(z[:, h * LANES:(h + 1) * LANES], cos, sin, first_half) for h in range(MXU_N // LANES)], axis=1)

    def cols(j):
        return slice(j * MXU_N, (j + 1) * MXU_N)

    def store_per_sequence(ref, zt):
        seq = ref.shape[-1]
        for s in range(ref.shape[0]):
            ref[s] = zt[:, s * seq:(s + 1) * seq]

    def store_q(j, z):
        q_ref[:, cols(j)] = rotate(_head_norm(z, seg_ref, qg_ref[...])).astype(BF16)

    def store_k(z):
        k = _head_norm(z, seg_ref, kg_ref[...])
        kt_ref[...] = rotate(k).T.astype(BF16)
        if not rope:
            store_per_sequence(ktf_ref, k.T)

    def store_v(z):
        v_ref[...] = z.astype(BF16)
        if not rope:
            store_per_sequence(vtf_ref, z.T)

    def store_conv_in(j, zb, zc, zx):
        w = cw_ref[:, cols(j)]
        y = zc * zx
        before = pltpu.roll(y, 1, axis=0)[HALO:HALO + tm]
        after = pltpu.roll(y, rows - 1, axis=0)[HALO:HALO + tm]
        if period < tm:
            t = jax.lax.broadcasted_iota(jnp.int32, (tm, 1), 0) % period
            before = jnp.where(t == 0, 0.0, before)
            after = jnp.where(t == period - 1, 0.0, after)
        conv = w[0:1] * before + w[1:2] * y[HALO:HALO + tm] + w[2:3] * after
        cin_ref[:, cols(j)] = (zb * conv).astype(BF16)

    def store_gate(ref, j, z):
        ref[:, cols(j)] = jax.nn.sigmoid(z).astype(BF16)

    work = [([(u, wq_ref, j * MXU_N)], functools.partial(store_q, j)) for j in range(ATT_WIDTH // MXU_N)]
    work += [([(u, w_ref, OFF_K)], store_k)]
    for j in range(D_MODEL // MXU_N):
        conv_operands = [(u, w_ref, OFF_B + j * MXU_N), (u_ext, w_ref, OFF_C + j * MXU_N), (u_ext, w_ref, OFF_X + j * MXU_N)]
        work += [(conv_operands, functools.partial(store_conv_in, j)),
                 ([(u, w_ref, OFF_GA + j * MXU_N)], functools.partial(store_gate, sa_ref, j)),
                 ([(u, w_ref, OFF_GC + j * MXU_N)], functools.partial(store_gate, sg_ref, j))]
    work += [([(u, w_ref, OFF_V)], store_v)]
    pending = None
    for operands, consume in work:
        products = [_dot(lhs, weights[:, lo:lo + MXU_N]) for lhs, weights, lo in operands]
        if pending is not None:
            pending[0](*pending[1])
        pending = (consume, products)
    pending[0](*pending[1])


def _in_proj(x, ada3, ada_row, seq_len, tm, wq, w_in, g_pre1, qg, kg, seg, conv_w, rope_tabs):
    n = x.shape[0]
    rope = rope_tabs is not None
    assert seq_len % tm == 0 or tm % seq_len == 0
    tiles_per_seq = max(seq_len // tm, 1)
    row = lambda i: (i, 0)
    prev, nxt = _halo_specs(tm, n, D_MODEL)
    in_specs = [
        pl.BlockSpec((tm, D_MODEL), row), prev, nxt,
        pl.BlockSpec((1, 1, D_MODEL), lambda i: (ada_row(i), 0, 0)),
        pl.BlockSpec((1, 1, D_MODEL), lambda i: (ada_row(i), 0, 1)),
        _resident((1, D_MODEL)),
        _resident((D_MODEL, ATT_WIDTH)),
        _resident((D_MODEL, IN_WIDTH)),
        _resident((1, MXU_N)),
        _resident((1, MXU_N)),
        _resident((MXU_N, MXU_N)),
        _resident((3, D_MODEL)),
    ]
    args = [x, x, x, ada3, ada3, g_pre1, wq, w_in, qg, kg, seg, conv_w]
    wide = jax.ShapeDtypeStruct((n, D_MODEL), BF16)
    wide_spec = pl.BlockSpec((tm, D_MODEL), row)
    out_shape = [wide, jax.ShapeDtypeStruct((KV_WIDTH, n), BF16), jax.ShapeDtypeStruct((n, KV_WIDTH), BF16)] + [wide] * 3
    out_specs = [wide_spec, pl.BlockSpec((KV_WIDTH, tm), lambda i: (0, i)), pl.BlockSpec((tm, KV_WIDTH), row)]
    out_specs += [wide_spec] * 3
    if rope:
        assert seq_len % tm == 0
        in_specs += [pl.BlockSpec((tm, LANES), lambda i: (i % tiles_per_seq, 0))] * 2
        args += list(rope_tabs)
    else:
        assert tm % seq_len == 0
        per_tile = tm // seq_len
        transposed = jax.ShapeDtypeStruct((n // seq_len, KV_WIDTH, seq_len), F32)
        out_shape += [transposed] * 2
        out_specs += [pl.BlockSpec((per_tile, KV_WIDTH, seq_len), lambda i: (i, 0, 0))] * 2
    return pl.pallas_call(
        functools.partial(_in_proj_kernel, rope, tiles_per_seq, min(seq_len, tm)),
        grid=(n // tm,),
        in_specs=in_specs,
        out_specs=out_specs,
        out_shape=out_shape,
        scratch_shapes=[pltpu.VMEM((tm + 2 * HALO, D_MODEL), BF16)],
        compiler_params=_params(1),
        name="in_proj_rope" if rope else "in_proj",
    )(*args)


def _scores(q_ref, kt_ref, col0, g):
    lo = (g // 2) * LANES
    q_pair = q_ref[0, :, col0 + lo:col0 + lo + LANES]
    slot = jax.lax.broadcasted_iota(jnp.int32, (1, LANES), 1) // HEAD_DIM
    q_one = jnp.where(slot == g % 2, q_pair, jnp.zeros_like(q_pair))
    return _dot(q_one, kt_ref[lo:lo + LANES, :])


def _group_of_lane():
    return jax.lax.broadcasted_iota(jnp.int32, (1, MXU_N), 1) // HEAD_DIM


def _sum_lane(g):
    return (HEAD_DIM * (g + 1)) % MXU_N


def _build_masked_v(pieces, vm_ref):
    lane = jax.lax.broadcasted_iota(jnp.int32, (1, MXU_N), 1)
    for row0, v in pieces:
        v = v.astype(F32)
        for g in range(N_KV_HEADS):
            ones_col = (lane == _sum_lane(g)).astype(F32)
            vm_ref[g, row0:row0 + v.shape[0], :] = jnp.where(_group_of_lane() == g, v, ones_col).astype(BF16)


def _unnormalised_probs(s):
    return jnp.exp2(s - jnp.max(s, axis=-1, keepdims=True)).astype(BF16)


def _weighted_values(probs, vm_ref):
    acc = None
    den = None
    for g in range(N_KV_HEADS):
        pv = _dot(probs[g], vm_ref[g])
        own = _group_of_lane() == g
        row_sum = pv[:, _sum_lane(g):_sum_lane(g) + 1]
        acc = pv if acc is None else jnp.where(own, pv, acc)
        den = row_sum if den is None else jnp.where(own, row_sum, den)
    return acc / den


def _attn_kernel(n_sets, q_ref, kt_ref, v_ref, o_ref, vm_ref):
    @pl.when((pl.program_id(1) == 0) & (pl.program_id(2) == 0))
    def _():
        _build_masked_v([(0, v_ref[...])], vm_ref)

    def probs_of(j):
        return [_unnormalised_probs(_scores(q_ref, kt_ref, j * MXU_N, g)) for g in range(N_KV_HEADS)]

    probs_next = probs_of(0)
    for j in range(n_sets):
        probs = probs_next
        if j + 1 < n_sets:
            probs_next = probs_of(j + 1)
        o_ref[0, :, j * MXU_N:(j + 1) * MXU_N] = _weighted_values(probs, vm_ref).astype(BF16)


def _attention(q, kt, v, tq, sets_per_step):
    bsz, n, _ = q.shape
    t = kt.shape[-1] // bsz
    width = sets_per_step * MXU_N
    return pl.pallas_call(
        functools.partial(_attn_kernel, sets_per_step),
        grid=(bsz, GROUP // sets_per_step, n // tq),
        in_specs=[
            pl.BlockSpec((1, tq, width), lambda b, j, i: (b, i, j)),
            pl.BlockSpec((KV_WIDTH, t), lambda b, j, i: (0, b)),
            pl.BlockSpec((t, KV_WIDTH), lambda b, j, i: (b, 0)),
        ],
        out_specs=pl.BlockSpec((1, tq, width), lambda b, j, i: (b, i, j)),
        out_shape=jax.ShapeDtypeStruct(q.shape, BF16),
        scratch_shapes=[pltpu.VMEM((N_KV_HEADS, t, MXU_N), BF16)],
        compiler_params=_params(3),
        name=f"attn_t{t}",
    )(q, kt, v)


def _attn_pipe_kernel(steps_per_batch, q_ref, ktc_ref, ktn_ref, vc_ref, vn_ref, o_ref, vm_ref, pa_ref, pb_ref, s_ref):
    t = pl.program_id(0)

    @pl.when(t == 0)
    def _():
        pb_ref[...] = jnp.ones_like(pb_ref)

    @pl.when((t == 0) | ((t - 1) % steps_per_batch == 0))
    def _():
        _build_masked_v([(0, vc_ref[0]), (vc_ref.shape[1], vn_ref[...])], vm_ref)

    cached_chunks = ktc_ref.shape[-1] // KEY_CHUNK
    n_chunks = cached_chunks + ktn_ref.shape[-1] // KEY_CHUNK
    slot_of_lane = jax.lax.broadcasted_iota(jnp.int32, (1, LANES), 1) // HEAD_DIM

    def keys(c):
        return slice(c * KEY_CHUNK, (c + 1) * KEY_CHUNK)

    def score_chunk(g, c, run_max):
        lo = (g // 2) * LANES
        q_pair = q_ref[0, :, lo:lo + LANES]
        q_one = jnp.where(slot_of_lane == g % 2, q_pair, jnp.zeros_like(q_pair))
        if c < cached_chunks:
            kt = ktc_ref[0, lo:lo + LANES, keys(c)]
        else:
            kt = ktn_ref[lo:lo + LANES, keys(c - cached_chunks)]
        s = _dot(q_one, kt)
        s_ref[g % 2, :, keys(c)] = s
        for k in range(KEY_CHUNK // LANES):
            part = s[:, k * LANES:(k + 1) * LANES]
            run_max = part if run_max is None else jnp.maximum(run_max, part)
        return run_max

    def step(p_new, p_old):
        run_max = None
        for c in range(n_chunks):
            run_max = score_chunk(0, c, run_max)
        out = None
        den = None
        for g in range(N_KV_HEADS):
            row_max = jnp.max(run_max, axis=-1, keepdims=True)
            run_max = None
            acc = None
            for c in range(n_chunks):
                p_new[g, :, keys(c)] = jnp.exp2(s_ref[g % 2, :, keys(c)] - row_max).astype(BF16)
                if g + 1 < N_KV_HEADS:
                    run_max = score_chunk(g + 1, c, run_max)
                pv = _dot(p_old[g, :, keys(c)], vm_ref[g, keys(c), :])
                acc = pv if acc is None else acc + pv
            own = _group_of_lane() == g
            row_sum = acc[:, _sum_lane(g):_sum_lane(g) + 1]
            out = acc if out is None else jnp.where(own, acc, out)
            den = row_sum if den is None else jnp.where(own, row_sum, den)
        o_ref[0] = (out / den).astype(BF16)

    @pl.when(t % 2 == 0)
    def _():
        step(pa_ref, pb_ref)

    @pl.when(t % 2 == 1)
    def _():
        step(pb_ref, pa_ref)


def _attention_pipelined(q, ktc, ktn, vc, vn, tq):
    bsz, n, _ = q.shape
    t_cached = ktc.shape[-1]
    assert t_cached % KEY_CHUNK == 0 and n % KEY_CHUNK == 0
    t_keys = t_cached + n
    tiles = n // tq
    steps_per_batch = GROUP * tiles
    n_items = bsz * steps_per_batch

    def item(t):
        return t // steps_per_batch, (t // tiles) % GROUP, t % tiles

    def cur(t):
        return item(jnp.minimum(t, n_items - 1))

    def prev(t):
        return item(jnp.clip(t - 1, 0, n_items - 1))

    def q_map(t):
        b, j, i = cur(t)
        return b, i, j

    def o_map(t):
        b, j, i = prev(t)
        return b, i, j

    return pl.pallas_call(
        functools.partial(_attn_pipe_kernel, steps_per_batch),
        grid=(n_items + 1,),
        in_specs=[
            pl.BlockSpec((1, tq, MXU_N), q_map),
            pl.BlockSpec((1, KV_WIDTH, t_cached), lambda t: (cur(t)[0], 0, 0)),
            pl.BlockSpec((KV_WIDTH, n), lambda t: (0, cur(t)[0])),
            pl.BlockSpec((1, t_cached, KV_WIDTH), lambda t: (prev(t)[0], 0, 0)),
            pl.BlockSpec((n, KV_WIDTH), lambda t: (prev(t)[0], 0)),
        ],
        out_specs=pl.BlockSpec((1, tq, MXU_N), o_map),
        out_shape=jax.ShapeDtypeStruct(q.shape, BF16),
        scratch_shapes=[pltpu.VMEM((N_KV_HEADS, t_keys, MXU_N), BF16),
                        pltpu.VMEM((N_KV_HEADS, tq, t_keys), BF16), pltpu.VMEM((N_KV_HEADS, tq, t_keys), BF16),
                        pltpu.VMEM((2, tq, t_keys), F32)],
        compiler_params=_params(1),
        name=f"attn_pipe_t{t_keys}",
    )(q, ktc, ktn, vc, vn)


def _mix_out_kernel(att_ref, cin_ref, sa_ref, sg_ref, h_ref, g1_ref, sh2_ref, sc2_ref, gpost_ref, gpre_ref, wa_ref, wc_ref,
                    wo_ref, h1_ref, u2_ref):
    tm = h_ref.shape[0]
    halves = [slice(0, tm // 2), slice(tm // 2, tm)]
    cnv = [_dot(cin_ref[r, :], wc_ref[...]) for r in halves]
    att = [_dot(att_ref[r, :], wa_ref[...]) for r in halves]

    mixed = []
    half = D_MODEL // 2
    for r, a, c in zip(halves, att, cnv):
        merged = (sa_ref[r, :].astype(F32) * a + sg_ref[r, :].astype(F32) * c).astype(BF16)
        mixed.append(jnp.concatenate([_dot(merged, wo_ref[:, n * half:(n + 1) * half]) for n in range(2)], axis=1))

    for r, mo in zip(halves, mixed):
        h1 = h_ref[r, :] + g1_ref[0] * (_rms_rows(mo) * gpost_ref[...])
        h1_ref[r, :] = h1
        u2 = _rms_rows(h1) * (gpre_ref[...] * (1.0 + sc2_ref[0])) + sh2_ref[0]
        u2_ref[r, :] = u2.astype(BF16)


def _halo_specs(tm, n_rows, width):
    per = tm // HALO
    last = n_rows // HALO - 1
    prev = pl.BlockSpec((HALO, width), lambda i: (jnp.maximum(i * per - 1, 0), 0))
    nxt = pl.BlockSpec((HALO, width), lambda i: (jnp.minimum((i + 1) * per, last), 0))
    return prev, nxt


def _mix_out(att, conv_in, sa, sg, h, ada3, ada_row, tm, g_post1, g_pre2, w_att_out, w_conv_out, w_o):
    n = h.shape[0]
    wide = pl.BlockSpec((tm, D_MODEL), lambda i: (i, 0))
    ada = lambda k: pl.BlockSpec((1, 1, D_MODEL), lambda i: (ada_row(i), 0, k))
    sq = (D_MODEL, D_MODEL)
    return pl.pallas_call(
        _mix_out_kernel,
        grid=(n // tm,),
        in_specs=[wide, wide, wide, wide, wide, ada(2), ada(3), ada(4), _resident((1, D_MODEL)), _resident((1, D_MODEL)),
                  _resident(sq), _resident(sq), _resident(sq)],
        out_specs=[wide, wide],
        out_shape=[jax.ShapeDtypeStruct((n, D_MODEL), F32), jax.ShapeDtypeStruct((n, D_MODEL), BF16)],
        compiler_params=_params(1),
        name=f"mix_out_t{tm}_n{n}",
    )(att, conv_in, sa, sg, h, ada3, ada3, ada3, g_post1, g_pre2, w_att_out, w_conv_out, w_o)


FF_CHUNKS = (4 * MXU_N, 4 * MXU_N, 3 * MXU_N)
assert sum(FF_CHUNKS) == D_FF


def _ffn_kernel(tiles_per_seq, pieces, u_ref, up_ref, un_ref, h_ref, g2_ref, gpost_ref, cw_ref, wup_ref, wdn_ref, o_ref,
                ext_ref):
    i = pl.program_id(0)
    tm = u_ref.shape[0]
    piece = tm // pieces
    has_prev = i % tiles_per_seq != 0
    has_next = i % tiles_per_seq != tiles_per_seq - 1
    zeros = jnp.zeros((HALO, D_MODEL), BF16)
    starts = [HALO + p * (piece + HALO) for p in range(pieces)]
    rows = starts[-1] + piece + HALO
    ext_ref[0:HALO, :] = jnp.where(has_prev, up_ref[...], zeros)
    for p, r0 in enumerate(starts):
        ext_ref[r0:r0 + piece, :] = u_ref[p * piece:(p + 1) * piece, :]
        if p + 1 < pieces:
            ext_ref[r0 + piece:r0 + piece + HALO, :] = zeros
    ext_ref[rows - HALO:, :] = jnp.where(has_next, un_ref[...], zeros)
    ext = ext_ref[...]

    def tile_rows(z):
        return jnp.concatenate([z[r0:r0 + piece] for r0 in starts], axis=0) if pieces > 1 else z[HALO:HALO + tm]

    def up(lo, width):
        return [_dot(ext, wup_ref[:, col:col + width]) for col in (lo, D_FF + lo)]

    def conv(z, col, width):
        w = cw_ref[:, col:col + width]
        before = tile_rows(pltpu.roll(z, 1, axis=0))
        after = tile_rows(pltpu.roll(z, rows - 1, axis=0))
        return w[0:1] * before + w[1:2] * tile_rows(z) + w[2:3] * after

    def down(lo, width, z_gate, z_val):
        gate = conv(z_gate, lo, width)
        val = conv(z_val, D_FF + lo, width)
        act = (gate * jax.nn.sigmoid(gate) * val).astype(BF16)
        half = D_MODEL // 2
        return [_dot(act, wdn_ref[lo:lo + width, n * half:(n + 1) * half]) for n in range(2)]

    firsts = [sum(FF_CHUNKS[:c]) for c in range(len(FF_CHUNKS))]
    ff = None
    z_next = up(firsts[0], FF_CHUNKS[0])
    for c, (lo, width) in enumerate(zip(firsts, FF_CHUNKS)):
        z_cur = z_next
        if c + 1 < len(FF_CHUNKS):
            z_next = up(firsts[c + 1], FF_CHUNKS[c + 1])
        parts = down(lo, width, *z_cur)
        ff = parts if ff is None else [a + b for a, b in zip(ff, parts)]
    ff = jnp.concatenate(ff, axis=1)

    o_ref[...] = h_ref[...] + g2_ref[0] * (_rms_rows(ff) * gpost_ref[...])


def _ffn(u2, h1, ada3, ada_row, seq_len, tm, g_post2, conv_ffn, w_up, w_down):
    n = h1.shape[0]
    assert seq_len % tm == 0 or tm % seq_len == 0
    tiles_per_seq = max(seq_len // tm, 1)
    pieces = max(tm // seq_len, 1)
    row = lambda i: (i, 0)
    wide = pl.BlockSpec((tm, D_MODEL), row)
    prev, nxt = _halo_specs(tm, n, D_MODEL)
    return pl.pallas_call(
        functools.partial(_ffn_kernel, tiles_per_seq, pieces),
        grid=(n // tm,),
        in_specs=[wide, prev, nxt, wide, pl.BlockSpec((1, 1, D_MODEL), lambda i: (ada_row(i), 0, 5)),
                  _resident((1, D_MODEL)), _resident((3, 2 * D_FF)),
                  _resident((D_MODEL, 2 * D_FF)), _resident((D_FF, D_MODEL))],
        out_specs=wide,
        out_shape=jax.ShapeDtypeStruct((n, D_MODEL), F32),
        scratch_shapes=[pltpu.VMEM((tm + (pieces + 1) * HALO, D_MODEL), BF16)],
        compiler_params=_params(1),
        name=f"ffn_s{seq_len}",
    )(u2, u2, u2, h1, ada3, g_post2, conv_ffn, w_up, w_down)


def _rope_tables(n):
    pos = np.arange(n)
    inv = np.power(ROPE_THETA, -np.arange(0, AXIS_DIM, 2, dtype=np.float64) / AXIS_DIM)
    ang_r = (pos // GRID_W)[:, None] * inv[None, :]
    ang_c = (pos % GRID_W)[:, None] * inv[None, :]
    ang = np.concatenate([ang_r, ang_r, ang_c, ang_c], axis=1)
    sign = np.tile(np.concatenate([-np.ones(AXIS_DIM // 2), np.ones(AXIS_DIM // 2)]), 2)
    reps = LANES // HEAD_DIM
    cos = np.tile(np.cos(ang), (1, reps)).astype(np.float32)
    sin = np.tile(np.sin(ang) * sign[None, :], (1, reps)).astype(np.float32)
    return jnp.asarray(cos), jnp.asarray(sin)


def _segment_mean_matrix():
    head = np.arange(MXU_N) // HEAD_DIM
    return jnp.asarray((head[:, None] == head[None, :]).astype(np.float32) / HEAD_DIM, dtype=BF16)


def _layer(x, ada3, ada_row, seq_len, tiles, w, rope_tabs, cache_kv):
    bsz = x.shape[0]
    n = bsz * seq_len
    xf = x.reshape(n, D_MODEL)
    tm_in, tq, sets_per_step, tm_mix, tm_ffn = tiles
    q, kt, v, conv_in, sa, sg, *transposed = _in_proj(xf, ada3, lambda i: ada_row(i, tm_in), seq_len, tm_in, w["wq"],
                                                      w["w_in"], w["g_pre1"], w["qg"], w["kg"], w["seg"], w["conv_w"],
                                                      rope_tabs)
    q3 = q.reshape(bsz, seq_len, ATT_WIDTH)
    if cache_kv is None:
        att = _attention(q3, kt, v, tq, sets_per_step)
    else:
        ck, cv = cache_kv
        ktc = ck.reshape(bsz, -1, KV_WIDTH).transpose(0, 2, 1).astype(BF16)
        vc = cv.reshape(bsz, -1, KV_WIDTH).astype(BF16)
        att = _attention_pipelined(q3, ktc, kt, vc, v, tq)
    att = att.reshape(n, ATT_WIDTH)
    h1, u2 = _mix_out(att, conv_in, sa, sg, xf, ada3, lambda i: ada_row(i, tm_mix), tm_mix, w["g_post1"], w["g_pre2"],
                      w["w_att_out"], w["w_conv_out"], w["w_o"])
    out = _ffn(u2, h1, ada3, lambda i: ada_row(i, tm_ffn), seq_len, tm_ffn, w["g_post2"], w["conv_ffn"], w["w_up"],
               w["w_down"])
    return (out.reshape(x.shape), *transposed)


def kernel(x_prompt, x_sample, cache_k, cache_v, c, c_ctx, w_ada, b_ada, g_pre1, g_post1, g_pre2, g_post2, w_in, q_norm,
           k_norm, w_att_out, conv_w, w_conv_out, w_o, w_up, conv_ffn, w_down):
    depth = w_in.shape[0]
    dec_batch, dec_seq, _ = x_sample.shape
    batch, seq, _ = x_prompt.shape
    rope_tabs = _rope_tables(dec_seq)
    seg = _segment_mean_matrix()
    cc = jnp.zeros((ADA_ROWS, D_MODEL), F32).at[0].set(c_ctx).at[1:1 + dec_batch].set(c)
    reps = MXU_N // HEAD_DIM

    h_p, h_s = x_prompt, x_sample
    new_ks, new_vs = [], []
    for i in range(depth):
        wq = w_in[i][:, :ATT_WIDTH].reshape(D_MODEL, N_KV_HEADS, GROUP, HEAD_DIM).transpose(0, 2, 1, 3)
        wa = w_att_out[i].reshape(N_KV_HEADS, GROUP, HEAD_DIM, D_MODEL).transpose(1, 0, 2, 3)
        w = {
            "g_pre1": g_pre1[i][None], "g_post1": g_post1[i][None], "g_pre2": g_pre2[i][None], "g_post2": g_post2[i][None],
            "wq": wq.reshape(D_MODEL, ATT_WIDTH).astype(BF16), "w_in": w_in[i].astype(BF16),
            "qg": jnp.tile(q_norm[i] * (HEAD_DIM ** -0.5 * LOG2E), reps)[None],
            "kg": jnp.tile(k_norm[i], reps)[None],
            "seg": seg,
            "w_att_out": wa.reshape(ATT_WIDTH, D_MODEL).astype(BF16), "conv_w": conv_w[i],
            "w_conv_out": w_conv_out[i].astype(BF16),
            "w_o": w_o[i].astype(BF16), "w_up": w_up[i].astype(BF16), "conv_ffn": conv_ffn[i],
            "w_down": w_down[i].astype(BF16),
        }
        ada3 = _ada(cc, w_ada[i], b_ada[i]).reshape(ADA_ROWS, N_ADA, D_MODEL).reshape(ADA_ROWS, 1, N_ADA * D_MODEL)
        context_tiles = _Tiles(ROW_TILE, seq, GROUP, ROW_TILE, ROW_TILE)
        h_p, kt_ctx, vt_ctx = _layer(h_p, ada3, lambda t, tm: 0, seq, context_tiles, w, None, None)
        new_ks.append(kt_ctx.reshape(batch, N_KV_HEADS, HEAD_DIM, seq).transpose(0, 3, 1, 2))
        new_vs.append(vt_ctx.reshape(batch, N_KV_HEADS, HEAD_DIM, seq).transpose(0, 3, 1, 2))
        latent_tiles = _Tiles(ROW_TILE, Q_TILE, None, ROW_TILE, ROW_TILE)
        (h_s,) = _layer(h_s, ada3, lambda t, tm: 1 + t // (dec_seq // tm), dec_seq, latent_tiles, w,
                        rope_tabs, (cache_k[:, i], cache_v[:, i]))
    return (h_p, h_s, jnp.stack(new_ks, axis=1), jnp.stack(new_vs, axis=1))
```

```python
import functools
from typing import NamedTuple, Optional

import numpy as np
import jax
import jax.numpy as jnp
from jax.experimental import pallas as pl
from jax.experimental.pallas import tpu as pltpu

D_MODEL = 1024
N_HEADS = 16
N_KV_HEADS = 4
HEAD_DIM = 64
GROUP = N_HEADS // N_KV_HEADS
ATT_WIDTH = N_HEADS * HEAD_DIM
KV_WIDTH = N_KV_HEADS * HEAD_DIM
D_FF = 2816
GRID_W = 64
ROPE_THETA = 10000.0
AXIS_DIM = HEAD_DIM // 2
N_ADA = 6
EPS = 1e-6
LOG2E = 1.4426950408889634

OFF_Q = 0
OFF_K = OFF_Q + ATT_WIDTH
OFF_V = OFF_K + KV_WIDTH
OFF_B = OFF_V + KV_WIDTH
OFF_C = OFF_B + D_MODEL
OFF_X = OFF_C + D_MODEL
OFF_GA = OFF_X + D_MODEL
OFF_GC = OFF_GA + D_MODEL
IN_WIDTH = OFF_GC + D_MODEL

LANES = 128
MXU_N = 256
HALO = 16
ADA_ROWS = 16
VMEM_LIMIT = 56 * 1024 * 1024
KEY_CHUNK = 256
ROW_TILE = 1024
Q_TILE = 512


class _Tiles(NamedTuple):
    in_proj: int
    attn_q: int
    attn_sets: Optional[int]
    mix_out: int
    ffn: int

BF16 = jnp.bfloat16
F32 = jnp.float32


def _dot(a, b):
    return jnp.dot(a, b, preferred_element_type=F32)


def _resident(shape):
    nd = len(shape)
    return pl.BlockSpec(shape, lambda *_: (0,) * nd, pipeline_mode=pl.Buffered(1))


def _params(n_axes):
    return pltpu.CompilerParams(dimension_semantics=("arbitrary",) * n_axes, vmem_limit_bytes=VMEM_LIMIT)


def _ada_kernel(c_ref, w_ref, b_ref, o_ref):
    c = c_ref[...]
    s = (c * jax.nn.sigmoid(c)).astype(BF16)
    o_ref[...] = _dot(s, w_ref[...].astype(BF16)) + b_ref[...]


def _ada(cc, w_ada, b_ada):
    n = w_ada.shape[1]
    tn = D_MODEL
    return pl.pallas_call(
        _ada_kernel,
        grid=(n // tn,),
        in_specs=[
            pl.BlockSpec((ADA_ROWS, D_MODEL), lambda j: (0, 0)),
            pl.BlockSpec((D_MODEL, tn), lambda j: (0, j)),
            pl.BlockSpec((1, tn), lambda j: (0, j)),
        ],
        out_specs=pl.BlockSpec((ADA_ROWS, tn), lambda j: (0, j)),
        out_shape=jax.ShapeDtypeStruct((ADA_ROWS, n), F32),
        compiler_params=_params(1),
        name="ada",
    )(cc, w_ada, b_ada.reshape(1, n))


def _rms_rows(x):
    return x * jax.lax.rsqrt(jnp.mean(x * x, axis=-1, keepdims=True) + EPS)


def _head_norm(z, seg_ref, gain):
    ms = _dot((z * z).astype(BF16), seg_ref[...])
    return z * jax.lax.rsqrt(ms + EPS) * gain


def _rope(z, cos, sin, first_half):
    partner = jnp.where(first_half, pltpu.roll(z, LANES - AXIS_DIM // 2, axis=1), pltpu.roll(z, AXIS_DIM // 2, axis=1))
    return z * cos + partner * sin


def _in_proj_kernel(rope, tiles_per_seq, period, x_ref, xp_ref, xn_ref, sh_ref, sc_ref, gpre_ref, wq_ref, w_ref, qg_ref,
                    kg_ref, seg_ref, cw_ref, *rest):
    if rope:
        cos_ref, sin_ref, q_ref, kt_ref, v_ref, cin_ref, sa_ref, sg_ref, ext_ref = rest
        cos = cos_ref[...]
        sin = sin_ref[...]
        lane = jax.lax.broadcasted_iota(jnp.int32, (1, LANES), 1)
        first_half = (lane % AXIS_DIM) < (AXIS_DIM // 2)
    else:
        q_ref, kt_ref, v_ref, cin_ref, sa_ref, sg_ref, ktf_ref, vtf_ref, ext_ref = rest

    i = pl.program_id(0)
    tm = x_ref.shape[0]
    rows = tm + 2 * HALO
    mod = gpre_ref[...] * (1.0 + sc_ref[0])

    def pre_norm(x):
        return (_rms_rows(x) * mod + sh_ref[0]).astype(BF16)

    has_prev = i % tiles_per_seq != 0
    has_next = i % tiles_per_seq != tiles_per_seq - 1
    zeros = jnp.zeros((HALO, D_MODEL), BF16)
    ext_ref[0:HALO, :] = jnp.where(has_prev, pre_norm(xp_ref[...]), zeros)
    ext_ref[HALO:HALO + tm, :] = pre_norm(x_ref[...])
    ext_ref[HALO + tm:, :] = jnp.where(has_next, pre_norm(xn_ref[...]), zeros)
    u = ext_ref[HALO:HALO + tm, :]
    u_ext = ext_ref[...]

    def rotate(z):
        if not rope:
            return z
        return jnp.concatenate(
            [_rope(z[:, h * LANES:(h + 1) * LANES], cos, sin, first_half) for h in range(MXU_N // LANES)], axis=1)

    def cols(j):
        return slice(j * MXU_N, (j + 1) * MXU_N)

    def store_per_sequence(ref, zt):
        seq = ref.shape[-1]
        for s in range(ref.shape[0]):
            ref[s] = zt[:, s * seq:(s + 1) * seq]

    def store_q(j, z):
        q_ref[:, cols(j)] = rotate(_head_norm(z, seg_ref, qg_ref[...])).astype(BF16)

    def store_k(z):
        k = _head_norm(z, seg_ref, kg_ref[...])
        kt_ref[...] = rotate(k).T.astype(BF16)
        if not rope:
            store_per_sequence(ktf_ref, k.T)

    def store_v(z):
        v_ref[...] = z.astype(BF16)
        if not rope:
            store_per_sequence(vtf_ref, z.T)

    def store_conv_in(j, zb, zc, zx):
        w = cw_ref[:, cols(j)]
        y = zc * zx
        before = pltpu.roll(y, 1, axis=0)[HALO:HALO + tm]
        after = pltpu.roll(y, rows - 1, axis=0)[HALO:HALO + tm]
        if period < tm:
            t = jax.lax.broadcasted_iota(jnp.int32, (tm, 1), 0) % period
            before = jnp.where(t == 0, 0.0, before)
            after = jnp.where(t == period - 1, 0.0, after)
        conv = w[0:1] * before + w[1:2] * y[HALO:HALO + tm] + w[2:3] * after
        cin_ref[:, cols(j)] = (zb * conv).astype(BF16)

    def store_gate(ref, j, z):
        ref[:, cols(j)] = jax.nn.sigmoid(z).astype(BF16)

    work = [([(u, wq_ref, j * MXU_N)], functools.partial(store_q, j)) for j in range(ATT_WIDTH // MXU_N)]
    work += [([(u, w_ref, OFF_K)], store_k)]
    for j in range(D_MODEL // MXU_N):
        conv_operands = [(u, w_ref, OFF_B + j * MXU_N), (u_ext, w_ref, OFF_C + j * MXU_N), (u_ext, w_ref, OFF_X + j * MXU_N)]
        work += [(conv_operands, functools.partial(store_conv_in, j)),
                 ([(u, w_ref, OFF_GA + j * MXU_N)], functools.partial(store_gate, sa_ref, j)),
                 ([(u, w_ref, OFF_GC + j * MXU_N)], functools.partial(store_gate, sg_ref, j))]
    work += [([(u, w_ref, OFF_V)], store_v)]
    pending = None
    for operands, consume in work:
        products = [_dot(lhs, weights[:, lo:lo + MXU_N]) for lhs, weights, lo in operands]
        if pending is not None:
            pending[0](*pending[1])
        pending = (consume, products)
    pending[0](*pending[1])


def _in_proj(x, ada3, ada_row, seq_len, tm, wq, w_in, g_pre1, qg, kg, seg, conv_w, rope_tabs):
    n = x.shape[0]
    rope = rope_tabs is not None
    assert seq_len % tm == 0 or tm % seq_len == 0
    tiles_per_seq = max(seq_len // tm, 1)
    row = lambda i: (i, 0)
    prev, nxt = _halo_specs(tm, n, D_MODEL)
    in_specs = [
        pl.BlockSpec((tm, D_MODEL), row), prev, nxt,
        pl.BlockSpec((1, 1, D_MODEL), lambda i: (ada_row(i), 0, 0)),
        pl.BlockSpec((1, 1, D_MODEL), lambda i: (ada_row(i), 0, 1)),
        _resident((1, D_MODEL)),
        _resident((D_MODEL, ATT_WIDTH)),
        _resident((D_MODEL, IN_WIDTH)),
        _resident((1, MXU_N)),
        _resident((1, MXU_N)),
        _resident((MXU_N, MXU_N)),
        _resident((3, D_MODEL)),
    ]
    args = [x, x, x, ada3, ada3, g_pre1, wq, w_in, qg, kg, seg, conv_w]
    wide = jax.ShapeDtypeStruct((n, D_MODEL), BF16)
    wide_spec = pl.BlockSpec((tm, D_MODEL), row)
    out_shape = [wide, jax.ShapeDtypeStruct((KV_WIDTH, n), BF16), jax.ShapeDtypeStruct((n, KV_WIDTH), BF16)] + [wide] * 3
    out_specs = [wide_spec, pl.BlockSpec((KV_WIDTH, tm), lambda i: (0, i)), pl.BlockSpec((tm, KV_WIDTH), row)]
    out_specs += [wide_spec] * 3
    if rope:
        assert seq_len % tm == 0
        in_specs += [pl.BlockSpec((tm, LANES), lambda i: (i % tiles_per_seq, 0))] * 2
        args += list(rope_tabs)
    else:
        assert tm % seq_len == 0
        per_tile = tm // seq_len
        transposed = jax.ShapeDtypeStruct((n // seq_len, KV_WIDTH, seq_len), F32)
        out_shape += [transposed] * 2
        out_specs += [pl.BlockSpec((per_tile, KV_WIDTH, seq_len), lambda i: (i, 0, 0))] * 2
    return pl.pallas_call(
        functools.partial(_in_proj_kernel, rope, tiles_per_seq, min(seq_len, tm)),
        grid=(n // tm,),
        in_specs=in_specs,
        out_specs=out_specs,
        out_shape=out_shape,
        scratch_shapes=[pltpu.VMEM((tm + 2 * HALO, D_MODEL), BF16)],
        compiler_params=_params(1),
        name="in_proj_rope" if rope else "in_proj",
    )(*args)


def _scores(q_ref, kt_ref, col0, g):
    lo = (g // 2) * LANES
    q_pair = q_ref[0, :, col0 + lo:col0 + lo + LANES]
    slot = jax.lax.broadcasted_iota(jnp.int32, (1, LANES), 1) // HEAD_DIM
    q_one = jnp.where(slot == g % 2, q_pair, jnp.zeros_like(q_pair))
    return _dot(q_one, kt_ref[lo:lo + LANES, :])


def _group_of_lane():
    return jax.lax.broadcasted_iota(jnp.int32, (1, MXU_N), 1) // HEAD_DIM


def _sum_lane(g):
    return (HEAD_DIM * (g + 1)) % MXU_N


def _build_masked_v(pieces, vm_ref):
    lane = jax.lax.broadcasted_iota(jnp.int32, (1, MXU_N), 1)
    for row0, v in pieces:
        v = v.astype(F32)
        for g in range(N_KV_HEADS):
            ones_col = (lane == _sum_lane(g)).astype(F32)
            vm_ref[g, row0:row0 + v.shape[0], :] = jnp.where(_group_of_lane() == g, v, ones_col).astype(BF16)


def _unnormalised_probs(s):
    return jnp.exp2(s - jnp.max(s, axis=-1, keepdims=True)).astype(BF16)


def _weighted_values(probs, vm_ref):
    acc = None
    den = None
    for g in range(N_KV_HEADS):
        pv = _dot(probs[g], vm_ref[g])
        own = _group_of_lane() == g
        row_sum = pv[:, _sum_lane(g):_sum_lane(g) + 1]
        acc = pv if acc is None else jnp.where(own, pv, acc)
        den = row_sum if den is None else jnp.where(own, row_sum, den)
    return acc / den


def _attn_kernel(n_sets, q_ref, kt_ref, v_ref, o_ref, vm_ref):
    @pl.when((pl.program_id(1) == 0) & (pl.program_id(2) == 0))
    def _():
        _build_masked_v([(0, v_ref[...])], vm_ref)

    def probs_of(j):
        return [_unnormalised_probs(_scores(q_ref, kt_ref, j * MXU_N, g)) for g in range(N_KV_HEADS)]

    probs_next = probs_of(0)
    for j in range(n_sets):
        probs = probs_next
        if j + 1 < n_sets:
            probs_next = probs_of(j + 1)
        o_ref[0, :, j * MXU_N:(j + 1) * MXU_N] = _weighted_values(probs, vm_ref).astype(BF16)


def _attention(q, kt, v, tq, sets_per_step):
    bsz, n, _ = q.shape
    t = kt.shape[-1] // bsz
    width = sets_per_step * MXU_N
    return pl.pallas_call(
        functools.partial(_attn_kernel, sets_per_step),
        grid=(bsz, GROUP // sets_per_step, n // tq),
        in_specs=[
            pl.BlockSpec((1, tq, width), lambda b, j, i: (b, i, j)),
            pl.BlockSpec((KV_WIDTH, t), lambda b, j, i: (0, b)),
            pl.BlockSpec((t, KV_WIDTH), lambda b, j, i: (b, 0)),
        ],
        out_specs=pl.BlockSpec((1, tq, width), lambda b, j, i: (b, i, j)),
        out_shape=jax.ShapeDtypeStruct(q.shape, BF16),
        scratch_shapes=[pltpu.VMEM((N_KV_HEADS, t, MXU_N), BF16)],
        compiler_params=_params(3),
        name=f"attn_t{t}",
    )(q, kt, v)


def _attn_pipe_kernel(steps_per_batch, q_ref, kc_ref, ktn_ref, vc_ref, vn_ref, o_ref, vm_ref, pa_ref, pb_ref, s_ref,
                      ktc_ref):
    t = pl.program_id(0)

    @pl.when(t % steps_per_batch == 0)
    def _():
        ktc_ref[...] = kc_ref[0].T.astype(BF16)

    @pl.when(t == 0)
    def _():
        pb_ref[...] = jnp.ones_like(pb_ref)

    @pl.when((t == 0) | ((t - 1) % steps_per_batch == 0))
    def _():
        _build_masked_v([(0, vc_ref[0]), (vc_ref.shape[1], vn_ref[...])], vm_ref)

    cached_chunks = ktc_ref.shape[-1] // KEY_CHUNK
    n_chunks = cached_chunks + ktn_ref.shape[-1] // KEY_CHUNK
    slot_of_lane = jax.lax.broadcasted_iota(jnp.int32, (1, LANES), 1) // HEAD_DIM

    def keys(c):
        return slice(c * KEY_CHUNK, (c + 1) * KEY_CHUNK)

    def score_chunk(g, c, run_max):
        lo = (g // 2) * LANES
        q_pair = q_ref[0, :, lo:lo + LANES]
        q_one = jnp.where(slot_of_lane == g % 2, q_pair, jnp.zeros_like(q_pair))
        if c < cached_chunks:
            kt = ktc_ref[lo:lo + LANES, keys(c)]
        else:
            kt = ktn_ref[lo:lo + LANES, keys(c - cached_chunks)]
        s = _dot(q_one, kt)
        s_ref[g % 2, :, keys(c)] = s
        for k in range(KEY_CHUNK // LANES):
            part = s[:, k * LANES:(k + 1) * LANES]
            run_max = part if run_max is None else jnp.maximum(run_max, part)
        return run_max

    def step(p_new, p_old):
        run_max = None
        for c in range(n_chunks):
            run_max = score_chunk(0, c, run_max)
        out = None
        den = None
        for g in range(N_KV_HEADS):
            row_max = jnp.max(run_max, axis=-1, keepdims=True)
            run_max = None
            acc = None
            for c in range(n_chunks):
                p_new[g, :, keys(c)] = jnp.exp2(s_ref[g % 2, :, keys(c)] - row_max).astype(BF16)
                if g + 1 < N_KV_HEADS:
                    run_max = score_chunk(g + 1, c, run_max)
                pv = _dot(p_old[g, :, keys(c)], vm_ref[g, keys(c), :])
                acc = pv if acc is None else acc + pv
            own = _group_of_lane() == g
            row_sum = acc[:, _sum_lane(g):_sum_lane(g) + 1]
            out = acc if out is None else jnp.where(own, acc, out)
            den = row_sum if den is None else jnp.where(own, row_sum, den)
        o_ref[0] = (out / den).astype(BF16)

    @pl.when(t % 2 == 0)
    def _():
        step(pa_ref, pb_ref)

    @pl.when(t % 2 == 1)
    def _():
        step(pb_ref, pa_ref)


def _attention_pipelined(q, kc, ktn, vc, vn, tq):
    bsz, n, _ = q.shape
    t_cached = kc.shape[1]
    assert t_cached % KEY_CHUNK == 0 and n % KEY_CHUNK == 0
    t_keys = t_cached + n
    tiles = n // tq
    steps_per_batch = GROUP * tiles
    n_items = bsz * steps_per_batch

    def item(t):
        return t // steps_per_batch, (t // tiles) % GROUP, t % tiles

    def cur(t):
        return item(jnp.minimum(t, n_items - 1))

    def prev(t):
        return item(jnp.clip(t - 1, 0, n_items - 1))

    def q_map(t):
        b, j, i = cur(t)
        return b, i, j

    def o_map(t):
        b, j, i = prev(t)
        return b, i, j

    return pl.pallas_call(
        functools.partial(_attn_pipe_kernel, steps_per_batch),
        grid=(n_items + 1,),
        in_specs=[
            pl.BlockSpec((1, tq, MXU_N), q_map),
            pl.BlockSpec((1, t_cached, KV_WIDTH), lambda t: (cur(t)[0], 0, 0)),
            pl.BlockSpec((KV_WIDTH, n), lambda t: (0, cur(t)[0])),
            pl.BlockSpec((1, t_cached, KV_WIDTH), lambda t: (prev(t)[0], 0, 0)),
            pl.BlockSpec((n, KV_WIDTH), lambda t: (prev(t)[0], 0)),
        ],
        out_specs=pl.BlockSpec((1, tq, MXU_N), o_map),
        out_shape=jax.ShapeDtypeStruct(q.shape, BF16),
        scratch_shapes=[pltpu.VMEM((N_KV_HEADS, t_keys, MXU_N), BF16),
                        pltpu.VMEM((N_KV_HEADS, tq, t_keys), BF16), pltpu.VMEM((N_KV_HEADS, tq, t_keys), BF16),
                        pltpu.VMEM((2, tq, t_keys), F32), pltpu.VMEM((KV_WIDTH, t_cached), BF16)],
        compiler_params=_params(1),
        name=f"attn_pipe_t{t_keys}",
    )(q, kc, ktn, vc, vn)


def _mix_out_kernel(att_ref, cin_ref, sa_ref, sg_ref, h_ref, g1_ref, sh2_ref, sc2_ref, gpost_ref, gpre_ref, wa_ref, wc_ref,
                    wo_ref, h1_ref, u2_ref):
    tm = h_ref.shape[0]
    halves = [slice(0, tm // 2), slice(tm // 2, tm)]
    cnv = [_dot(cin_ref[r, :], wc_ref[...]) for r in halves]
    att = [_dot(att_ref[r, :], wa_ref[...]) for r in halves]

    mixed = []
    half = D_MODEL // 2
    for r, a, c in zip(halves, att, cnv):
        merged = (sa_ref[r, :].astype(F32) * a + sg_ref[r, :].astype(F32) * c).astype(BF16)
        mixed.append(jnp.concatenate([_dot(merged, wo_ref[:, n * half:(n + 1) * half]) for n in range(2)], axis=1))

    for r, mo in zip(halves, mixed):
        h1 = h_ref[r, :] + g1_ref[0] * (_rms_rows(mo) * gpost_ref[...])
        h1_ref[r, :] = h1
        u2 = _rms_rows(h1) * (gpre_ref[...] * (1.0 + sc2_ref[0])) + sh2_ref[0]
        u2_ref[r, :] = u2.astype(BF16)


def _halo_specs(tm, n_rows, width):
    per = tm // HALO
    last = n_rows // HALO - 1
    prev = pl.BlockSpec((HALO, width), lambda i: (jnp.maximum(i * per - 1, 0), 0))
    nxt = pl.BlockSpec((HALO, width), lambda i: (jnp.minimum((i + 1) * per, last), 0))
    return prev, nxt


def _mix_out(att, conv_in, sa, sg, h, ada3, ada_row, tm, g_post1, g_pre2, w_att_out, w_conv_out, w_o):
    n = h.shape[0]
    wide = pl.BlockSpec((tm, D_MODEL), lambda i: (i, 0))
    ada = lambda k: pl.BlockSpec((1, 1, D_MODEL), lambda i: (ada_row(i), 0, k))
    sq = (D_MODEL, D_MODEL)
    return pl.pallas_call(
        _mix_out_kernel,
        grid=(n // tm,),
        in_specs=[wide, wide, wide, wide, wide, ada(2), ada(3), ada(4), _resident((1, D_MODEL)), _resident((1, D_MODEL)),
                  _resident(sq), _resident(sq), _resident(sq)],
        out_specs=[wide, wide],
        out_shape=[jax.ShapeDtypeStruct((n, D_MODEL), F32), jax.ShapeDtypeStruct((n, D_MODEL), BF16)],
        compiler_params=_params(1),
        name=f"mix_out_t{tm}_n{n}",
    )(att, conv_in, sa, sg, h, ada3, ada3, ada3, g_post1, g_pre2, w_att_out, w_conv_out, w_o)


FF_CHUNKS = (4 * MXU_N, 4 * MXU_N, 3 * MXU_N)
assert sum(FF_CHUNKS) == D_FF


def _ffn_kernel(tiles_per_seq, pieces, u_ref, up_ref, un_ref, h_ref, g2_ref, gpost_ref, cw_ref, wup_ref, wdn_ref, o_ref,
                ext_ref):
    i = pl.program_id(0)
    tm = u_ref.shape[0]
    piece = tm // pieces
    has_prev = i % tiles_per_seq != 0
    has_next = i % tiles_per_seq != tiles_per_seq - 1
    zeros = jnp.zeros((HALO, D_MODEL), BF16)
    starts = [HALO + p * (piece + HALO) for p in range(pieces)]
    rows = starts[-1] + piece + HALO
    ext_ref[0:HALO, :] = jnp.where(has_prev, up_ref[...], zeros)
    for p, r0 in enumerate(starts):
        ext_ref[r0:r0 + piece, :] = u_ref[p * piece:(p + 1) * piece, :]
        if p + 1 < pieces:
            ext_ref[r0 + piece:r0 + piece + HALO, :] = zeros
    ext_ref[rows - HALO:, :] = jnp.where(has_next, un_ref[...], zeros)
    ext = ext_ref[...]

    def tile_rows(z):
        return jnp.concatenate([z[r0:r0 + piece] for r0 in starts], axis=0) if pieces > 1 else z[HALO:HALO + tm]

    def up(lo, width):
        return [_dot(ext, wup_ref[:, col:col + width]) for col in (lo, D_FF + lo)]

    def conv(z, col, width):
        w = cw_ref[:, col:col + width]
        before = tile_rows(pltpu.roll(z, 1, axis=0))
        after = tile_rows(pltpu.roll(z, rows - 1, axis=0))
        return w[0:1] * before + w[1:2] * tile_rows(z) + w[2:3] * after

    def down(lo, width, z_gate, z_val):
        gate = conv(z_gate, lo, width)
        val = conv(z_val, D_FF + lo, width)
        act = (gate * jax.nn.sigmoid(gate) * val).astype(BF16)
        half = D_MODEL // 2
        return [_dot(act, wdn_ref[lo:lo + width, n * half:(n + 1) * half]) for n in range(2)]

    firsts = [sum(FF_CHUNKS[:c]) for c in range(len(FF_CHUNKS))]
    ff = None
    z_next = up(firsts[0], FF_CHUNKS[0])
    for c, (lo, width) in enumerate(zip(firsts, FF_CHUNKS)):
        z_cur = z_next
        if c + 1 < len(FF_CHUNKS):
            z_next = up(firsts[c + 1], FF_CHUNKS[c + 1])
        parts = down(lo, width, *z_cur)
        ff = parts if ff is None else [a + b for a, b in zip(ff, parts)]
    ff = jnp.concatenate(ff, axis=1)

    o_ref[...] = h_ref[...] + g2_ref[0] * (_rms_rows(ff) * gpost_ref[...])


def _ffn(u2, h1, ada3, ada_row, seq_len, tm, g_post2, conv_ffn, w_up, w_down):
    n = h1.shape[0]
    assert seq_len % tm == 0 or tm % seq_len == 0
    tiles_per_seq = max(seq_len // tm, 1)
    pieces = max(tm // seq_len, 1)
    row = lambda i: (i, 0)
    wide = pl.BlockSpec((tm, D_MODEL), row)
    prev, nxt = _halo_specs(tm, n, D_MODEL)
    return pl.pallas_call(
        functools.partial(_ffn_kernel, tiles_per_seq, pieces),
        grid=(n // tm,),
        in_specs=[wide, prev, nxt, wide, pl.BlockSpec((1, 1, D_MODEL), lambda i: (ada_row(i), 0, 5)),
                  _resident((1, D_MODEL)), _resident((3, 2 * D_FF)),
                  _resident((D_MODEL, 2 * D_FF)), _resident((D_FF, D_MODEL))],
        out_specs=wide,
        out_shape=jax.ShapeDtypeStruct((n, D_MODEL), F32),
        scratch_shapes=[pltpu.VMEM((tm + (pieces + 1) * HALO, D_MODEL), BF16)],
        compiler_params=_params(1),
        name=f"ffn_s{seq_len}",
    )(u2, u2, u2, h1, ada3, g_post2, conv_ffn, w_up, w_down)


def _rope_tables(n):
    pos = np.arange(n)
    inv = np.power(ROPE_THETA, -np.arange(0, AXIS_DIM, 2, dtype=np.float64) / AXIS_DIM)
    ang_r = (pos // GRID_W)[:, None] * inv[None, :]
    ang_c = (pos % GRID_W)[:, None] * inv[None, :]
    ang = np.concatenate([ang_r, ang_r, ang_c, ang_c], axis=1)
    sign = np.tile(np.concatenate([-np.ones(AXIS_DIM // 2), np.ones(AXIS_DIM // 2)]), 2)
    reps = LANES // HEAD_DIM
    cos = np.tile(np.cos(ang), (1, reps)).astype(np.float32)
    sin = np.tile(np.sin(ang) * sign[None, :], (1, reps)).astype(np.float32)
    return jnp.asarray(cos), jnp.asarray(sin)


def _segment_mean_matrix():
    head = np.arange(MXU_N) // HEAD_DIM
    return jnp.asarray((head[:, None] == head[None, :]).astype(np.float32) / HEAD_DIM, dtype=BF16)


def _layer(x, ada3, ada_row, seq_len, tiles, w, rope_tabs, cache_kv):
    bsz = x.shape[0]
    n = bsz * seq_len
    xf = x.reshape(n, D_MODEL)
    tm_in, tq, sets_per_step, tm_mix, tm_ffn = tiles
    q, kt, v, conv_in, sa, sg, *transposed = _in_proj(xf, ada3, lambda i: ada_row(i, tm_in), seq_len, tm_in, w["wq"],
                                                      w["w_in"], w["g_pre1"], w["qg"], w["kg"], w["seg"], w["conv_w"],
                                                      rope_tabs)
    q3 = q.reshape(bsz, seq_len, ATT_WIDTH)
    if cache_kv is None:
        att = _attention(q3, kt, v, tq, sets_per_step)
    else:
        ck, cv = cache_kv
        att = _attention_pipelined(q3, ck.reshape(bsz, -1, KV_WIDTH), kt, cv.reshape(bsz, -1, KV_WIDTH), v, tq)
    att = att.reshape(n, ATT_WIDTH)
    h1, u2 = _mix_out(att, conv_in, sa, sg, xf, ada3, lambda i: ada_row(i, tm_mix), tm_mix, w["g_post1"], w["g_pre2"],
                      w["w_att_out"], w["w_conv_out"], w["w_o"])
    out = _ffn(u2, h1, ada3, lambda i: ada_row(i, tm_ffn), seq_len, tm_ffn, w["g_post2"], w["conv_ffn"], w["w_up"],
               w["w_down"])
    return (out.reshape(x.shape), *transposed)


def kernel(x_prompt, x_sample, cache_k, cache_v, c, c_ctx, w_ada, b_ada, g_pre1, g_post1, g_pre2, g_post2, w_in, q_norm,
           k_norm, w_att_out, conv_w, w_conv_out, w_o, w_up, conv_ffn, w_down):
    depth = w_in.shape[0]
    dec_batch, dec_seq, _ = x_sample.shape
    batch, seq, _ = x_prompt.shape
    rope_tabs = _rope_tables(dec_seq)
    seg = _segment_mean_matrix()
    cc = jnp.zeros((ADA_ROWS, D_MODEL), F32).at[0].set(c_ctx).at[1:1 + dec_batch].set(c)
    reps = MXU_N // HEAD_DIM

    h_p, h_s = x_prompt, x_sample
    new_ks, new_vs = [], []
    for i in range(depth):
        wq = w_in[i][:, :ATT_WIDTH].reshape(D_MODEL, N_KV_HEADS, GROUP, HEAD_DIM).transpose(0, 2, 1, 3)
        wa = w_att_out[i].reshape(N_KV_HEADS, GROUP, HEAD_DIM, D_MODEL).transpose(1, 0, 2, 3)
        w = {
            "g_pre1": g_pre1[i][None], "g_post1": g_post1[i][None], "g_pre2": g_pre2[i][None], "g_post2": g_post2[i][None],
            "wq": wq.reshape(D_MODEL, ATT_WIDTH).astype(BF16), "w_in": w_in[i].astype(BF16),
            "qg": jnp.tile(q_norm[i] * (HEAD_DIM ** -0.5 * LOG2E), reps)[None],
            "kg": jnp.tile(k_norm[i], reps)[None],
            "seg": seg,
            "w_att_out": wa.reshape(ATT_WIDTH, D_MODEL).astype(BF16), "conv_w": conv_w[i],
            "w_conv_out": w_conv_out[i].astype(BF16),
            "w_o": w_o[i].astype(BF16), "w_up": w_up[i].astype(BF16), "conv_ffn": conv_ffn[i],
            "w_down": w_down[i].astype(BF16),
        }
        ada3 = _ada(cc, w_ada[i], b_ada[i]).reshape(ADA_ROWS, N_ADA, D_MODEL).reshape(ADA_ROWS, 1, N_ADA * D_MODEL)
        context_tiles = _Tiles(ROW_TILE, seq, GROUP, ROW_TILE, ROW_TILE)
        h_p, kt_ctx, vt_ctx = _layer(h_p, ada3, lambda t, tm: 0, seq, context_tiles, w, None, None)
        new_ks.append(kt_ctx.reshape(batch, N_KV_HEADS, HEAD_DIM, seq).transpose(0, 3, 1, 2))
        new_vs.append(vt_ctx.reshape(batch, N_KV_HEADS, HEAD_DIM, seq).transpose(0, 3, 1, 2))
        latent_tiles = _Tiles(ROW_TILE, Q_TILE, None, ROW_TILE, ROW_TILE)
        (h_s,) = _layer(h_s, ada3, lambda t, tm: 1 + t // (dec_seq // tm), dec_seq, latent_tiles, w,
                        rope_tabs, (cache_k[:, i], cache_v[:, i]))
    return (h_p, h_s, jnp.stack(new_ks, axis=1), jnp.stack(new_vs, axis=1))
```
